```python
import jax, jax.numpy as jnp
from jax import lax
import numpy as np

D_MODEL = 2048
BATCH = 16
SEQ = 256
DEPTH = 2
DEC_BATCH = 4
DEC_SEQ = 1024
PAST_LEN = 256

GRID_W = 64
N_HEADS = 16
HEAD_DIM = 64
D_NA = N_HEADS * HEAD_DIM
D_F = D_MODEL // 2
N_FGROUPS = 4
F_GROUP = D_F // N_FGROUPS
KH_MAX = 8
KW = 16
Q_BLOCK_W = 16
K_BLOCK_W = KW + Q_BLOCK_W
D_FF = -(-8 * D_MODEL // (3 * 256)) * 256
D_IN = 3 * D_NA + D_F
EPS = 1e-6

kernel_name = "hybrid_natten_fnet_diffusion_step"


def rms_norm(x, g):
    xf = x.astype(jnp.float32)
    y = xf * lax.rsqrt(jnp.mean(xf * xf, axis=-1, keepdims=True) + EPS)
    return (y * g.astype(jnp.float32)).astype(x.dtype)


def modulation(cond, w_mod, b_mod):
    m = jax.nn.silu(cond) @ w_mod + b_mod
    return jnp.split(m[..., None, :], 6, axis=-1)


def project_in(h, w_in, q_g, k_g):
    p = h @ w_in
    q, k, v, u = jnp.split(p, [D_NA, 2 * D_NA, 3 * D_NA], axis=-1)
    shp = h.shape[:-1] + (N_HEADS, HEAD_DIM)
    q = rms_norm(q.reshape(shp), q_g)
    k = rms_norm(k.reshape(shp), k_g)
    return q, k, v.reshape(shp), u


def fourier_mix(u):
    b, t, _ = u.shape
    ug = u.reshape(b, t, N_FGROUPS, F_GROUP).astype(jnp.float32)
    f = jnp.fft.fft2(ug, axes=(1, 3), norm="ortho").real
    return f.reshape(b, t, D_F).astype(u.dtype)


def context_attention(q, k, v):
    s = jnp.einsum('bqhd,bkhd->bhqk', q, k).astype(jnp.float32) * (HEAD_DIM ** -0.5)
    p = jax.nn.softmax(s, axis=-1).astype(v.dtype)
    return jnp.einsum('bhqk,bkhd->bqhd', p, v)


def neighbourhood_attention(q, k, v, k_ctx, v_ctx, rpb):
    b, t, h, d = q.shape
    rows = t // GRID_W
    kh = min(KH_MAX, rows)
    nqb = GRID_W // Q_BLOCK_W
    r = jnp.arange(rows)
    row_idx = jnp.clip(r - kh // 2, 0, rows - kh)[:, None] + jnp.arange(kh)
    c0 = jnp.arange(nqb) * Q_BLOCK_W
    col_idx = jnp.clip(c0 - KW // 2, 0, GRID_W - K_BLOCK_W)[:, None] + jnp.arange(K_BLOCK_W)
    qc = c0[:, None] + jnp.arange(Q_BLOCK_W)
    win_start = jnp.clip(qc - KW // 2, 0, GRID_W - KW)[..., None]
    kc = col_idx[:, None, :]
    in_win = (kc >= win_start) & (kc < win_start + KW)
    dr = row_idx - r[:, None] + (KH_MAX - 1)
    dc = jnp.clip(kc - qc[..., None] + (KW - 1), 0, 2 * KW - 2)
    bias = rpb[:, dr[:, None, None, :, None], dc[None, :, :, None, :]].astype(jnp.float32)
    bias = jnp.where(in_win[None, None, :, :, None, :], bias, -jnp.inf)
    nk = kh * K_BLOCK_W
    bias = bias.reshape(h, rows, nqb, Q_BLOCK_W, nk)

    qg = q.reshape(b, rows, nqb, Q_BLOCK_W, h, d)
    kg = k.reshape(b, rows, GRID_W, h, d)
    vg = v.reshape(b, rows, GRID_W, h, d)
    ri = row_idx[:, None, :, None]
    ci = col_idx[None, :, None, :]
    kb = kg[:, ri, ci].reshape(b, rows, nqb, nk, h, d)
    vb = vg[:, ri, ci].reshape(b, rows, nqb, nk, h, d)

    scale = HEAD_DIM ** -0.5
    s_loc = jnp.einsum('brnqhd,brnkhd->bhrnqk', qg, kb).astype(jnp.float32) * scale + bias
    s_ctx = jnp.einsum('brnqhd,blhd->bhrnql', qg, k_ctx).astype(jnp.float32) * scale
    p = jax.nn.softmax(jnp.concatenate([s_loc, s_ctx], axis=-1), axis=-1).astype(v.dtype)
    o = (jnp.einsum('bhrnqk,brnkhd->brnqhd', p[..., :nk], vb)
         + jnp.einsum('bhrnql,blhd->brnqhd', p[..., nk:], v_ctx))
    return o.reshape(b, t, h, d)


def token_mixing(h, attn, u, w_na_proj, w_fnet_proj, w_gate, w_o):
    b, t = h.shape[:2]
    na = attn.reshape(b, t, D_NA) @ w_na_proj
    fn = fourier_mix(u) @ w_fnet_proj
    g_na, g_fn = jnp.split(jax.nn.sigmoid(h @ w_gate), 2, axis=-1)
    return (g_na * na + g_fn * fn) @ w_o


def swiglu(h, w_gate_up, w_down):
    a, g = jnp.split(h @ w_gate_up, 2, axis=-1)
    return (jax.nn.silu(g) * a) @ w_down


def trunk_layer(x, cond, lw, attend):
    (w_mod_l, b_mod_l, n1, n2, w_in_l, qg_l, kg_l,
     w_na_proj_l, w_fnet_proj_l, w_gate_l, w_o_l, w_gu_l, w_down_l) = lw
    sh1, sc1, g1, sh2, sc2, g2 = modulation(cond, w_mod_l, b_mod_l)
    h = rms_norm(x, n1) * (1 + sc1) + sh1
    q, k, v, u = project_in(h, w_in_l, qg_l, kg_l)
    attn = attend(q, k, v)
    x = x + g1 * token_mixing(h, attn, u, w_na_proj_l, w_fnet_proj_l, w_gate_l, w_o_l)
    h2 = rms_norm(x, n2) * (1 + sc2) + sh2
    x = x + g2 * swiglu(h2, w_gu_l, w_down_l)
    return x, k, v


def setup_inputs(seed: int = 0) -> dict:
    key = jax.random.key(seed)
    ks = jax.random.split(key, 24)
    nrm = lambda k, shape, s: jax.random.normal(k, shape, jnp.float32) * s
    return {
        "x_prompt": nrm(ks[0], (BATCH, SEQ, D_MODEL), 1.0),
        "x_sample": nrm(ks[1], (DEC_BATCH, DEC_SEQ, D_MODEL), 1.0),
        "cache_k": nrm(ks[2], (DEC_BATCH, DEPTH, PAST_LEN, N_HEADS, HEAD_DIM), 1.0),
        "cache_v": nrm(ks[3], (DEC_BATCH, DEPTH, PAST_LEN, N_HEADS, HEAD_DIM), 1.0),
        "c": nrm(ks[4], (DEC_BATCH, D_MODEL), 1.0),
        "c_ctx": nrm(ks[5], (D_MODEL,), 1.0),
        "w_mod": nrm(ks[6], (DEPTH, D_MODEL, 6 * D_MODEL), 0.5 * D_MODEL ** -0.5),
        "b_mod": nrm(ks[7], (DEPTH, 6 * D_MODEL), 0.02),
        "norm1_g": 1.0 + nrm(ks[8], (DEPTH, D_MODEL), 0.01),
        "norm2_g": 1.0 + nrm(ks[9], (DEPTH, D_MODEL), 0.01),
        "w_in": nrm(ks[10], (DEPTH, D_MODEL, D_IN), D_MODEL ** -0.5),
        "q_norm_g": 1.0 + nrm(ks[11], (DEPTH, HEAD_DIM), 0.01),
        "k_norm_g": 1.0 + nrm(ks[12], (DEPTH, HEAD_DIM), 0.01),
        "rpb": nrm(ks[13], (DEPTH, N_HEADS, 2 * KH_MAX - 1, 2 * KW - 1), 0.1),
        "w_na_proj": nrm(ks[14], (DEPTH, D_NA, D_MODEL), D_NA ** -0.5),
        "w_fnet_proj": nrm(ks[15], (DEPTH, D_F, D_MODEL), D_F ** -0.5),
        "w_gate": nrm(ks[16], (DEPTH, D_MODEL, 2 * D_MODEL), D_MODEL ** -0.5),
        "w_o": nrm(ks[17], (DEPTH, D_MODEL, D_MODEL), D_MODEL ** -0.5),
        "w_gate_up": nrm(ks[18], (DEPTH, D_MODEL, 2 * D_FF), D_MODEL ** -0.5),
        "w_down": nrm(ks[19], (DEPTH, D_FF, D_MODEL), D_FF ** -0.5),
    }


def reference(x_prompt, x_sample, cache_k, cache_v, c, c_ctx, w_mod, b_mod, norm1_g, norm2_g,
              w_in, q_norm_g, k_norm_g, rpb, w_na_proj, w_fnet_proj, w_gate, w_o, w_gate_up, w_down):
    y_prompt = x_prompt
    y_sample = x_sample
    new_ks = []
    new_vs = []
    for l in range(DEPTH):
        lw = (w_mod[l], b_mod[l], norm1_g[l], norm2_g[l], w_in[l], q_norm_g[l], k_norm_g[l],
              w_na_proj[l], w_fnet_proj[l], w_gate[l], w_o[l], w_gate_up[l], w_down[l])
        y_prompt, k_ctx, v_ctx = trunk_layer(y_prompt, c_ctx, lw, context_attention)
        new_ks.append(k_ctx)
        new_vs.append(v_ctx)
        ck = cache_k[:, l]
        cv = cache_v[:, l]
        rpb_l = rpb[l]
        attend_lat = lambda q, k, v, ck=ck, cv=cv, rpb_l=rpb_l: neighbourhood_attention(q, k, v, ck, cv, rpb_l)
        y_sample, _, _ = trunk_layer(y_sample, c, lw, attend_lat)
    new_k = jnp.stack(new_ks, axis=1)
    new_v = jnp.stack(new_vs, axis=1)
    return (y_prompt, y_sample, new_k, new_v)
```

```python
import functools

import numpy as np
import jax
import jax.numpy as jnp
from jax import lax
from jax.experimental import pallas as pl
from jax.experimental.pallas import tpu as pltpu

F32 = jnp.float32
BF16 = jnp.bfloat16

D_MODEL = 2048
BATCH = 16
SEQ = 256
DEPTH = 2
DEC_BATCH = 4
DEC_SEQ = 1024
PAST_LEN = 256
GRID_W = 64
ROWS = DEC_SEQ // GRID_W
N_HEADS = 16
HEAD_DIM = 64
D_NA = N_HEADS * HEAD_DIM
D_F = D_MODEL // 2
N_FGROUPS = 4
F_GROUP = D_F // N_FGROUPS
KH = 8
KW = 16
D_FF = 5632
D_IN = 3 * D_NA + D_F
EPS = 1e-6
N_DR = 2 * KH - 1
N_DC = 2 * KW - 1

N_CTX = BATCH * SEQ
N_LAT = DEC_BATCH * DEC_SEQ
N_TOK = N_CTX + N_LAT

V7X_LANES = 128
V7X_VMEM_LIMIT = 56 * 1024 * 1024

TM = 1024
N_MT = N_TOK // TM
N_CTX_MT = N_CTX // TM
NEG_INF = float("-inf")


def _params(n_axes):
    return pltpu.CompilerParams(dimension_semantics=("arbitrary",) * n_axes,
                                vmem_limit_bytes=V7X_VMEM_LIMIT)


def _mod_row(m):
    return jnp.where(m < N_CTX_MT, 0, m - (N_CTX_MT - 1))


def _mod_spec(layer, chunk, width, col_map):
    return pl.BlockSpec((None, None, None, 1, width),
                        lambda m, s: (layer, _mod_row(m), chunk, 0, col_map(s)))


def _sigmoid(z):
    return 1.0 / (1.0 + jnp.exp(-z))


def _bdot(a, b):
    return jnp.dot(a, b, preferred_element_type=F32)


MOD_TN = 1024


def _mod_kernel(cond_ref, w_ref, b_ref, o_ref):
    cnd = cond_ref[...]
    s = (cnd * _sigmoid(cnd)).astype(BF16)
    o_ref[...] = _bdot(s, w_ref[...].astype(BF16)) + b_ref[...]


def _modulation(cond8, w_mod, b_mod):
    n_blk = 6 * D_MODEL // MOD_TN
    return pl.pallas_call(
        _mod_kernel,
        grid=(DEPTH, n_blk),
        in_specs=[
            pl.BlockSpec((8, D_MODEL), lambda l, n: (0, 0)),
            pl.BlockSpec((None, D_MODEL, MOD_TN), lambda l, n: (l, 0, n)),
            pl.BlockSpec((None, 1, MOD_TN), lambda l, n: (l, 0, n)),
        ],
        out_specs=pl.BlockSpec((None, 8, MOD_TN), lambda l, n: (l, 0, n)),
        out_shape=jax.ShapeDtypeStruct((DEPTH, 8, 6 * D_MODEL), F32),
        compiler_params=_params(2),
        name="modulation",
    )(cond8, w_mod, b_mod.reshape(DEPTH, 1, 6 * D_MODEL))


def _bias_onehots():
    width = GRID_W * V7X_LANES
    oh_l = np.zeros((N_DC + 1, width), np.float32)
    oh_r = np.zeros((N_DC + 1, width), np.float32)
    mask = np.full((1, width), -np.inf, np.float32)
    for c in range(GRID_W):
        ws = min(max(c - KW // 2, 0), GRID_W - KW)
        for kc in range(ws, ws + KW):
            e = kc - c + KW - 1
            oh_l[e, c * V7X_LANES + kc] = 1.0
            oh_r[e, c * V7X_LANES + GRID_W + kc] = 1.0
            mask[0, c * V7X_LANES + kc] = 0.0
            mask[0, c * V7X_LANES + GRID_W + kc] = 0.0
    mask_last = mask.copy().reshape(GRID_W, V7X_LANES)
    mask_last[:, GRID_W:] = -np.inf
    return oh_l, oh_r, mask, mask_last.reshape(1, width)


def _split3(x):
    hi = x.astype(BF16)
    r1 = x - hi.astype(F32)
    mid = r1.astype(BF16)
    lo = (r1 - mid.astype(F32)).astype(BF16)
    return hi, mid, lo


def _bias_kernel(r1_ref, r2_ref, sel_ref, ohl_ref, ohr_ref, mask_ref, maskl_ref, o_ref):
    acc = None
    for r_ref, oh_ref in ((r1_ref, ohl_ref), (r2_ref, ohr_ref)):
        oh = oh_ref[...]
        for piece in _split3(r_ref[...]):
            t = _bdot(piece, oh)
            acc = t if acc is None else acc + t
    mask = jnp.where(sel_ref[...] > 0.5, maskl_ref[...], mask_ref[...])
    o_ref[...] = acc + mask


def _bias_tables(rpb):
    oh_l, oh_r, mask, mask_last = _bias_onehots()
    rows = N_HEADS * N_DR
    width = GRID_W * V7X_LANES
    pad = jnp.zeros((DEPTH, N_HEADS, N_DR, 1), F32)
    r1 = jnp.concatenate([rpb, pad], axis=-1).reshape(DEPTH * rows, N_DC + 1)
    nxt = jnp.concatenate([rpb[:, :, 1:], jnp.zeros((DEPTH, N_HEADS, 1, N_DC), F32)], axis=2)
    r2 = jnp.concatenate([nxt, pad], axis=-1).reshape(DEPTH * rows, N_DC + 1)
    sel = np.zeros((DEPTH, N_HEADS, N_DR, 1), np.float32)
    sel[:, :, N_DR - 1] = 1.0
    sel = jnp.asarray(sel.reshape(DEPTH * rows, 1))
    const = lambda shape: pl.BlockSpec(shape, lambda l: (0, 0))
    out = pl.pallas_call(
        _bias_kernel,
        grid=(DEPTH,),
        in_specs=[
            pl.BlockSpec((rows, N_DC + 1), lambda l: (l, 0)),
            pl.BlockSpec((rows, N_DC + 1), lambda l: (l, 0)),
            pl.BlockSpec((rows, 1), lambda l: (l, 0)),
            const((N_DC + 1, width)), const((N_DC + 1, width)),
            const((1, width)), const((1, width)),
        ],
        out_specs=pl.BlockSpec((rows, width), lambda l: (l, 0)),
        out_shape=jax.ShapeDtypeStruct((DEPTH * rows, width), F32),
        compiler_params=_params(1),
        name="bias_tables",
    )(r1, r2, sel, jnp.asarray(oh_l).astype(BF16), jnp.asarray(oh_r).astype(BF16),
      jnp.asarray(mask), jnp.asarray(mask_last))
    return out.reshape(DEPTH, N_HEADS, N_DR, GRID_W, V7X_LANES)


NORM_RC = 128


def _norm_mod_rows(x_ref, g_ref, sc_ref, sh_ref, dst_refs, rows):
    gain = g_ref[...]
    scale1 = 1.0 + sc_ref[...]
    shift = sh_ref[...]

    def body(i, carry):
        r0 = pl.multiple_of(i * NORM_RC, NORM_RC)
        x = x_ref[pl.ds(r0, NORM_RC), :]
        ms = jnp.mean(x * x, axis=-1, keepdims=True)
        y = x * lax.rsqrt(ms + EPS)
        hb = ((y * gain) * scale1 + shift).astype(BF16)
        for dst in dst_refs:
            dst[pl.ds(r0, NORM_RC), :] = hb
        return carry

    lax.fori_loop(0, rows // NORM_RC, body, 0)


IN_TN = 512
N_IN_BLK = D_IN // IN_TN
IN_QK_BLKS = 2 * D_NA // IN_TN
IN_QKV_BLKS = 3 * D_NA // IN_TN
IN_Q_BLKS = D_NA // IN_TN


def _inproj_kernel(x_ref, g_ref, sc_ref, sh_ref, w_ref, qkg_ref, ones_ref, cs_ref,
                   h_ref, p_ref, ac_ref, as_ref, kv_ref, h_s):
    m = pl.program_id(0)
    n = pl.program_id(1)

    @pl.when(n == 0)
    def _():
        _norm_mod_rows(x_ref, g_ref, sc_ref, sh_ref, (h_s, h_ref), TM)

    y = _bdot(h_s[...], w_ref[...].astype(BF16))

    @pl.when(n < IN_QK_BLKS)
    def _():
        ss = _bdot((y * y).astype(BF16), ones_ref[...])
        yn = y * lax.rsqrt(ss * (1.0 / HEAD_DIM) + EPS) * qkg_ref[...]
        p_ref[...] = yn.astype(BF16)

        @pl.when(jnp.logical_and(n >= IN_Q_BLKS, m < N_CTX_MT))
        def _():
            kv_ref[...] = yn

    @pl.when(jnp.logical_and(n >= IN_QK_BLKS, n < IN_QKV_BLKS))
    def _():
        p_ref[...] = y.astype(BF16)

        @pl.when(m < N_CTX_MT)
        def _():
            kv_ref[...] = y

    @pl.when(n >= IN_QKV_BLKS)
    def _():
        cs = cs_ref[...]
        for gg in range(IN_TN // F_GROUP):
            a = _bdot(y[:, gg * F_GROUP:(gg + 1) * F_GROUP].astype(BF16), cs)
            ac_ref[:, gg * F_GROUP:(gg + 1) * F_GROUP] = a[:, :F_GROUP].astype(BF16)
            as_ref[:, gg * F_GROUP:(gg + 1) * F_GROUP] = a[:, F_GROUP:].astype(BF16)


def _in_projection(layer, x, mod, norm_g, w_in, qkg, ones_bd, cs):
    u_blk = lambda n: jnp.clip(n - IN_QKV_BLKS, 0, 1)

    def kv_map(m, n):
        last = IN_QKV_BLKS - IN_Q_BLKS - 1
        ctx = m < N_CTX_MT
        return (jnp.where(ctx, m, N_CTX_MT - 1),
                jnp.where(ctx, jnp.clip(n - IN_Q_BLKS, 0, last), last))

    return pl.pallas_call(
        _inproj_kernel,
        grid=(N_MT, N_IN_BLK),
        in_specs=[
            pl.BlockSpec((TM, D_MODEL), lambda m, n: (m, 0)),
            pl.BlockSpec((None, 1, D_MODEL), lambda m, n: (layer, 0, 0)),
            _mod_spec(layer, 1, D_MODEL, lambda n: 0),
            _mod_spec(layer, 0, D_MODEL, lambda n: 0),
            pl.BlockSpec((None, D_MODEL, IN_TN), lambda m, n: (layer, 0, n)),
            pl.BlockSpec((None, None, 1, IN_TN),
                         lambda m, n: (layer, jnp.clip(n // IN_Q_BLKS, 0, 1), 0, 0)),
            pl.BlockSpec((IN_TN, IN_TN), lambda m, n: (0, 0)),
            pl.BlockSpec((F_GROUP, 2 * F_GROUP), lambda m, n: (0, 0)),
        ],
        out_specs=[
            pl.BlockSpec((TM, D_MODEL), lambda m, n: (m, 0)),
            pl.BlockSpec((TM, IN_TN), lambda m, n: (m, jnp.minimum(n, IN_QKV_BLKS - 1))),
            pl.BlockSpec((TM, IN_TN), lambda m, n: (m, u_blk(n))),
            pl.BlockSpec((TM, IN_TN), lambda m, n: (m, u_blk(n))),
            pl.BlockSpec((TM, IN_TN), kv_map),
        ],
        out_shape=[
            jax.ShapeDtypeStruct((N_TOK, D_MODEL), BF16),
            jax.ShapeDtypeStruct((N_TOK, 3 * D_NA), BF16),
            jax.ShapeDtypeStruct((N_TOK, D_F), BF16),
            jax.ShapeDtypeStruct((N_TOK, D_F), BF16),
            jax.ShapeDtypeStruct((N_CTX, 2 * D_NA), F32),
        ],
        scratch_shapes=[pltpu.VMEM((TM, D_MODEL), BF16)],
        compiler_params=_params(2),
        name=f"in_projection_l{layer}",
    )(x, norm_g.reshape(DEPTH, 1, D_MODEL), mod, mod, w_in, qkg, ones_bd, cs)


def _stack_heads(q):
    lane = lax.broadcasted_iota(jnp.int32, q.shape, 1)
    zero = jnp.zeros_like(q)
    return jnp.concatenate([jnp.where(lane < HEAD_DIM, q, zero),
                            jnp.where(lane >= HEAD_DIM, q, zero)], axis=0)


def _unstack_heads(o2):
    rows = o2.shape[0] // 2
    lane = lax.broadcasted_iota(jnp.int32, (rows, V7X_LANES), 1)
    return jnp.where(lane < HEAD_DIM, o2[:rows], o2[rows:])


def _qk(q2, k):
    return lax.dot_general(q2, k, (((1,), (1,)), ((), ())), preferred_element_type=F32)


def _ctx_attn_kernel(q_ref, k_ref, v_ref, o_ref):
    for p in range(N_HEADS // 2):
        cols = slice(p * V7X_LANES, (p + 1) * V7X_LANES)
        q2 = _stack_heads(q_ref[:, cols])
        s = _qk(q2, k_ref[:, cols])
        e = jnp.exp(s - jnp.max(s, axis=-1, keepdims=True))
        prob = e * (1.0 / jnp.sum(e, axis=-1, keepdims=True))
        o2 = _bdot(prob.astype(BF16), v_ref[:, cols])
        o_ref[:, cols] = _unstack_heads(o2).astype(BF16)


def _context_attention(p):
    return pl.pallas_call(
        _ctx_attn_kernel,
        grid=(BATCH,),
        in_specs=[pl.BlockSpec((SEQ, D_NA), lambda b: (b, 0)),
                  pl.BlockSpec((SEQ, D_NA), lambda b: (b, 1)),
                  pl.BlockSpec((SEQ, D_NA), lambda b: (b, 2))],
        out_specs=pl.BlockSpec((SEQ, D_NA), lambda b: (b, 0)),
        out_shape=jax.ShapeDtypeStruct((N_TOK, D_NA), BF16),
        compiler_params=_params(1),
        name="context_attention",
    )(p, p, p)


NA_QROWS = 4
NA_KROWS = 12
NA_KSTART = (0, 0, 4, 4)


def _na_bias(tbl_ref, group):
    ks = NA_KSTART[group]
    lane = lax.broadcasted_iota(jnp.int32, (GRID_W, V7X_LANES), 1)
    ninf = jnp.full((GRID_W, V7X_LANES), NEG_INF, F32)
    strips = []
    for hh in range(2):
        for rr in range(NA_QROWS):
            r = NA_QROWS * group + rr
            rs = min(max(r - KH // 2, 0), ROWS - KH)
            tiles = []
            for a in range(NA_KROWS // 2):
                kk0 = ks + 2 * a
                ok0 = rs <= kk0 < rs + KH
                ok1 = rs <= kk0 + 1 < rs + KH
                dr0 = kk0 - r + KH - 1
                if ok0 and ok1:
                    t = tbl_ref[hh, dr0]
                elif ok0:
                    t = jnp.where(lane < GRID_W, tbl_ref[hh, dr0], ninf)
                elif ok1:
                    t = jnp.where(lane >= GRID_W, tbl_ref[hh, dr0], ninf)
                else:
                    t = ninf
                tiles.append(t)
            strips.append(jnp.concatenate(tiles, axis=1))
    return jnp.concatenate(strips, axis=0)


def _lat_attn_kernel(q_ref, k_ref, v_ref, kc_ref, vc_ref, tbl_ref, prev_ref, o_ref):
    del prev_ref
    kc = kc_ref[...].astype(BF16)
    vc = vc_ref[...].astype(BF16)
    qrows = NA_QROWS * GRID_W
    for g in range(ROWS // NA_QROWS):
        k0 = NA_KSTART[g] * GRID_W
        q2 = _stack_heads(q_ref[g * qrows:(g + 1) * qrows, :])
        s_loc = _qk(q2, k_ref[k0:k0 + NA_KROWS * GRID_W, :]) + _na_bias(tbl_ref, g)
        s_ctx = _qk(q2, kc)
        mx = jnp.maximum(jnp.max(s_loc, axis=-1, keepdims=True), jnp.max(s_ctx, axis=-1, keepdims=True))
        e_loc = jnp.exp(s_loc - mx)
        e_ctx = jnp.exp(s_ctx - mx)
        inv = 1.0 / (jnp.sum(e_loc, axis=-1, keepdims=True) + jnp.sum(e_ctx, axis=-1, keepdims=True))
        o2 = (_bdot((e_loc * inv).astype(BF16), v_ref[k0:k0 + NA_KROWS * GRID_W, :])
              + _bdot((e_ctx * inv).astype(BF16), vc))
        o_ref[g * qrows:(g + 1) * qrows, :] = _unstack_heads(o2).astype(BF16)


def _latent_attention(layer, p, cache_k, cache_v, tbl, attn):
    n_pairs = N_HEADS // 2
    row = lambda b: N_CTX // DEC_SEQ + b
    cache_spec = pl.BlockSpec((None, None, PAST_LEN, V7X_LANES), lambda b, h: (b, layer, 0, h))
    return pl.pallas_call(
        _lat_attn_kernel,
        grid=(DEC_BATCH, n_pairs),
        in_specs=[
            pl.BlockSpec((DEC_SEQ, V7X_LANES), lambda b, h: (row(b), h)),
            pl.BlockSpec((DEC_SEQ, V7X_LANES), lambda b, h: (row(b), n_pairs + h)),
            pl.BlockSpec((DEC_SEQ, V7X_LANES), lambda b, h: (row(b), 2 * n_pairs + h)),
            cache_spec, cache_spec,
            pl.BlockSpec((None, 2, N_DR, GRID_W, V7X_LANES), lambda b, h: (layer, h, 0, 0, 0)),
            pl.BlockSpec(memory_space=pl.ANY),
        ],
        out_specs=pl.BlockSpec((DEC_SEQ, V7X_LANES), lambda b, h: (row(b), h)),
        out_shape=jax.ShapeDtypeStruct((N_TOK, D_NA), BF16),
        input_output_aliases={6: 0},
        compiler_params=_params(2),
        name=f"latent_attention_l{layer}",
    )(p, p, p, cache_k, cache_v, tbl, attn)


def _dft_tables(n, scale):
    j = np.arange(n, dtype=np.int64)
    ang = 2.0 * np.pi * ((j[:, None] * j[None, :]) % n).astype(np.float64) / n
    return (np.cos(ang) * scale).astype(np.float32), (np.sin(ang) * scale).astype(np.float32)


def _pos_dft_kernel(ct_ref, st_ref, ac_ref, as_ref, *rest):
    o_ref = rest[-1]
    o_ref[...] = (_bdot(ct_ref[...], ac_ref[...]) - _bdot(st_ref[...], as_ref[...])).astype(BF16)


def _position_dft(seq, n_batch, row0, a_cos, a_sin, prev=None):
    ct, st = _dft_tables(seq, seq ** -0.5)
    blk = pl.BlockSpec((seq, D_F), lambda b: (row0 + b, 0))
    tab = pl.BlockSpec((seq, seq), lambda b: (0, 0))
    in_specs = [tab, tab, blk, blk]
    args = [jnp.asarray(ct).astype(BF16), jnp.asarray(st).astype(BF16), a_cos, a_sin]
    aliases = {}
    if prev is not None:
        in_specs.append(pl.BlockSpec(memory_space=pl.ANY))
        args.append(prev)
        aliases = {4: 0}
    return pl.pallas_call(
        _pos_dft_kernel,
        grid=(n_batch,),
        in_specs=in_specs,
        out_specs=blk,
        out_shape=jax.ShapeDtypeStruct((N_TOK, D_F), BF16),
        input_output_aliases=aliases,
        compiler_params=_params(1),
        name=f"position_dft_{seq}",
    )(*args)


MIX_TC = 256
N_MIX_BLK = D_MODEL // MIX_TC
OUT_TN = 256
N_OUT_BLK = D_MODEL // OUT_TN


def _mix_kernel(h_ref, a_ref, f_ref, wgn_ref, wgf_ref, wna_ref, wf_ref, wo_ref, x_ref, g1_ref,
                o_ref, mix_s):
    s = pl.program_id(1)

    @pl.when(s < N_MIX_BLK)
    def _():
        h = h_ref[...]
        g_na = _sigmoid(_bdot(h, wgn_ref[...].astype(BF16)))
        g_fn = _sigmoid(_bdot(h, wgf_ref[...].astype(BF16)))
        na = _bdot(a_ref[...], wna_ref[...].astype(BF16))
        fn = _bdot(f_ref[...], wf_ref[...].astype(BF16))
        mix_s[s] = (g_na * na + g_fn * fn).astype(BF16)

    @pl.when(s >= N_MIX_BLK)
    def _():
        acc = None
        for c in range(N_MIX_BLK):
            t = _bdot(mix_s[c], wo_ref[c * MIX_TC:(c + 1) * MIX_TC, :].astype(BF16))
            acc = t if acc is None else acc + t
        o_ref[...] = x_ref[...] + g1_ref[...] * acc


def _token_mixing(layer, h, attn, f, x, mod, w_gate, w_na, w_f, w_o):
    mix_blk = lambda s: jnp.minimum(s, N_MIX_BLK - 1)
    out_blk = lambda s: jnp.clip(s - N_MIX_BLK, 0, N_OUT_BLK - 1)
    return pl.pallas_call(
        _mix_kernel,
        grid=(N_MT, N_MIX_BLK + N_OUT_BLK),
        in_specs=[
            pl.BlockSpec((TM, D_MODEL), lambda m, s: (m, 0)),
            pl.BlockSpec((TM, D_NA), lambda m, s: (m, 0)),
            pl.BlockSpec((TM, D_F), lambda m, s: (m, 0)),
            pl.BlockSpec((None, D_MODEL, MIX_TC), lambda m, s: (layer, 0, mix_blk(s))),
            pl.BlockSpec((None, D_MODEL, MIX_TC), lambda m, s: (layer, 0, N_MIX_BLK + mix_blk(s))),
            pl.BlockSpec((None, D_NA, MIX_TC), lambda m, s: (layer, 0, mix_blk(s))),
            pl.BlockSpec((None, D_F, MIX_TC), lambda m, s: (layer, 0, mix_blk(s))),
            pl.BlockSpec((None, D_MODEL, OUT_TN), lambda m, s: (layer, 0, out_blk(s))),
            pl.BlockSpec((TM, OUT_TN), lambda m, s: (m, out_blk(s))),
            _mod_spec(layer, 2, OUT_TN, out_blk),
        ],
        out_specs=pl.BlockSpec((TM, OUT_TN), lambda m, s: (m, out_blk(s))),
        out_shape=jax.ShapeDtypeStruct((N_TOK, D_MODEL), F32),
        scratch_shapes=[pltpu.VMEM((N_MIX_BLK, TM, MIX_TC), BF16)],
        compiler_params=_params(2),
        name=f"token_mixing_l{layer}",
    )(h, attn, f, w_gate, w_gate, w_na, w_f, w_o, x, mod)


NORM_TM = 512


def _norm2_kernel(x_ref, g_ref, sc_ref, sh_ref, o_ref):
    _norm_mod_rows(x_ref, g_ref, sc_ref, sh_ref, (o_ref,), NORM_TM)


def _norm_mod(layer, x, mod, norm_g):
    per_row = DEC_SEQ // NORM_TM
    n_ctx = N_CTX // NORM_TM
    row = lambda m: jnp.where(m < n_ctx, 0, 1 + (m - n_ctx) // per_row)
    spec = lambda chunk: pl.BlockSpec((None, None, None, 1, D_MODEL),
                                      lambda m: (layer, row(m), chunk, 0, 0))
    return pl.pallas_call(
        _norm2_kernel,
        grid=(N_TOK // NORM_TM,),
        in_specs=[pl.BlockSpec((NORM_TM, D_MODEL), lambda m: (m, 0)),
                  pl.BlockSpec((None, 1, D_MODEL), lambda m: (layer, 0, 0)),
                  spec(4), spec(3)],
        out_specs=pl.BlockSpec((NORM_TM, D_MODEL), lambda m: (m, 0)),
        out_shape=jax.ShapeDtypeStruct((N_TOK, D_MODEL), BF16),
        compiler_params=_params(1),
        name=f"norm_mod2_l{layer}",
    )(x, norm_g.reshape(DEPTH, 1, D_MODEL), mod, mod)


FF_TC = 256
N_FF_BLK = D_FF // FF_TC
DOWN_TN = 256
N_DOWN_BLK = D_MODEL // DOWN_TN


def _ffn_kernel(h_ref, wa_ref, wg_ref, wd_ref, x_ref, g2_ref, o_ref, act_s):
    s = pl.program_id(1)

    @pl.when(s < N_FF_BLK)
    def _():
        h = h_ref[...]
        a = _bdot(h, wa_ref[...].astype(BF16))
        g = _bdot(h, wg_ref[...].astype(BF16))
        act_s[s] = ((g * _sigmoid(g)) * a).astype(BF16)

    @pl.when(s >= N_FF_BLK)
    def _():
        acc = None
        for j in range(N_FF_BLK):
            t = _bdot(act_s[j], wd_ref[j * FF_TC:(j + 1) * FF_TC, :].astype(BF16))
            acc = t if acc is None else acc + t
        o_ref[...] = x_ref[...] + g2_ref[...] * acc


def _ffn(layer, h2, x, mod, w_gate_up, w_down):
    ff_blk = lambda s: jnp.minimum(s, N_FF_BLK - 1)
    out_blk = lambda s: jnp.clip(s - N_FF_BLK, 0, N_DOWN_BLK - 1)
    return pl.pallas_call(
        _ffn_kernel,
        grid=(N_MT, N_FF_BLK + N_DOWN_BLK),
        in_specs=[
            pl.BlockSpec((TM, D_MODEL), lambda m, s: (m, 0)),
            pl.BlockSpec((None, D_MODEL, FF_TC), lambda m, s: (layer, 0, ff_blk(s))),
            pl.BlockSpec((None, D_MODEL, FF_TC), lambda m, s: (layer, 0, N_FF_BLK + ff_blk(s))),
            pl.BlockSpec((None, D_FF, DOWN_TN), lambda m, s: (layer, 0, out_blk(s))),
            pl.BlockSpec((TM, DOWN_TN), lambda m, s: (m, out_blk(s))),
            _mod_spec(layer, 5, DOWN_TN, out_blk),
        ],
        out_specs=pl.BlockSpec((TM, DOWN_TN), lambda m, s: (m, out_blk(s))),
        out_shape=jax.ShapeDtypeStruct((N_TOK, D_MODEL), F32),
        scratch_shapes=[pltpu.VMEM((N_FF_BLK, TM, FF_TC), BF16)],
        compiler_params=_params(2),
        name=f"ffn_l{layer}",
    )(h2, w_gate_up, w_gate_up, w_down, x, mod)


def kernel(x_prompt, x_sample, cache_k, cache_v, c, c_ctx, w_mod, b_mod, norm1_g, norm2_g,
           w_in, q_norm_g, k_norm_g, rpb, w_na_proj, w_fnet_proj, w_gate, w_o, w_gate_up, w_down):
    x = jnp.concatenate([x_prompt.reshape(N_CTX, D_MODEL), x_sample.reshape(N_LAT, D_MODEL)], axis=0)
    cond8 = jnp.concatenate([c_ctx[None, :], c, jnp.zeros((8 - 1 - DEC_BATCH, D_MODEL), F32)], axis=0)
    mod = _modulation(cond8, w_mod, b_mod).reshape(DEPTH, 8, 6, 1, D_MODEL)
    tbl = _bias_tables(rpb)

    reps = IN_TN // HEAD_DIM
    qkg = jnp.stack([jnp.tile(q_norm_g * (HEAD_DIM ** -0.5), (1, reps)),
                     jnp.tile(k_norm_g, (1, reps))], axis=1).reshape(DEPTH, 2, 1, IN_TN)
    head_id = np.arange(IN_TN) // HEAD_DIM
    ones_bd = jnp.asarray((head_id[:, None] == head_id[None, :]).astype(np.float32)).astype(BF16)
    cc, sc = _dft_tables(F_GROUP, F_GROUP ** -0.5)
    cs = jnp.asarray(np.concatenate([cc, sc], axis=1)).astype(BF16)

    ck = cache_k.reshape(DEC_BATCH, DEPTH, PAST_LEN, D_NA)
    cv = cache_v.reshape(DEC_BATCH, DEPTH, PAST_LEN, D_NA)

    new_ks, new_vs = [], []
    for l in range(DEPTH):
        h, p, a_cos, a_sin, kv = _in_projection(l, x, mod, norm1_g, w_in, qkg, ones_bd, cs)
        attn = _context_attention(p)
        attn = _latent_attention(l, p, ck, cv, tbl, attn)
        f = _position_dft(SEQ, BATCH, 0, a_cos, a_sin)
        f = _position_dft(DEC_SEQ, DEC_BATCH, N_CTX // DEC_SEQ, a_cos, a_sin, prev=f)
        x1 = _token_mixing(l, h, attn, f, x, mod, w_gate, w_na_proj, w_fnet_proj, w_o)
        h2 = _norm_mod(l, x1, mod, norm2_g)
        x = _ffn(l, h2, x1, mod, w_gate_up, w_down)
        new_ks.append(kv[:, :D_NA].reshape(BATCH, SEQ, N_HEADS, HEAD_DIM))
        new_vs.append(kv[:, D_NA:].reshape(BATCH, SEQ, N_HEADS, HEAD_DIM))

    y_prompt = x[:N_CTX].reshape(BATCH, SEQ, D_MODEL)
    y_sample = x[N_CTX:].reshape(DEC_BATCH, DEC_SEQ, D_MODEL)
    return (y_prompt, y_sample, jnp.stack(new_ks, axis=1), jnp.stack(new_vs, axis=1))
```

```python
import numpy as np
import jax
import jax.numpy as jnp
from jax import lax
from jax.experimental import pallas as pl
from jax.experimental.pallas import tpu as pltpu

F32 = jnp.float32
BF16 = jnp.bfloat16

D_MODEL = 2048
BATCH = 16
SEQ = 256
DEPTH = 2
DEC_BATCH = 4
DEC_SEQ = 1024
PAST_LEN = 256
GRID_W = 64
ROWS = DEC_SEQ // GRID_W
N_HEADS = 16
HEAD_DIM = 64
D_NA = N_HEADS * HEAD_DIM
D_F = D_MODEL // 2
N_FGROUPS = 4
F_GROUP = D_F // N_FGROUPS
KH = 8
KW = 16
D_FF = 5632
D_IN = 3 * D_NA + D_F
EPS = 1e-6
N_DR = 2 * KH - 1
N_DC = 2 * KW - 1

N_TOK = BATCH * SEQ
assert N_TOK == DEC_BATCH * DEC_SEQ

V7X_LANES = 128
V7X_VMEM_LIMIT = 56 * 1024 * 1024

TM = 1024
N_MT = N_TOK // TM
NEG_INF = float("-inf")
CTX, LAT = "ctx", "lat"


def _params(n_axes):
    return pltpu.CompilerParams(dimension_semantics=("arbitrary",) * n_axes,
                                vmem_limit_bytes=V7X_VMEM_LIMIT)


def _mod_row(path, m, tile):
    return 0 if path == CTX else 1 + m // (DEC_SEQ // tile)


def _mod_spec(layer, path, chunk, width, col_map):
    return pl.BlockSpec((None, None, None, 1, width),
                        lambda m, s: (layer, _mod_row(path, m, TM), chunk, 0, col_map(s)))


def _resident(shape, index_map):
    return pl.BlockSpec(shape, index_map, pipeline_mode=pl.Buffered(1))


def _sigmoid(z):
    return 1.0 / (1.0 + jnp.exp(-z))


def _bdot(a, b):
    return jnp.dot(a, b, preferred_element_type=F32)


MOD_TN = 1024


def _mod_kernel(cond_ref, w_ref, b_ref, o_ref):
    cnd = cond_ref[...]
    s = (cnd * _sigmoid(cnd)).astype(BF16)
    o_ref[...] = _bdot(s, w_ref[...].astype(BF16)) + b_ref[...]


def _modulation(cond8, w_mod, b_mod):
    n_blk = 6 * D_MODEL // MOD_TN
    return pl.pallas_call(
        _mod_kernel,
        grid=(DEPTH, n_blk),
        in_specs=[
            pl.BlockSpec((8, D_MODEL), lambda l, n: (0, 0)),
            pl.BlockSpec((None, D_MODEL, MOD_TN), lambda l, n: (l, 0, n)),
            pl.BlockSpec((None, 1, MOD_TN), lambda l, n: (l, 0, n)),
        ],
        out_specs=pl.BlockSpec((None, 8, MOD_TN), lambda l, n: (l, 0, n)),
        out_shape=jax.ShapeDtypeStruct((DEPTH, 8, 6 * D_MODEL), F32),
        compiler_params=_params(2),
        name="modulation",
    )(cond8, w_mod, b_mod.reshape(DEPTH, 1, 6 * D_MODEL))


def _bias_onehots():
    width = GRID_W * V7X_LANES
    oh_l = np.zeros((N_DC + 1, width), np.float32)
    oh_r = np.zeros((N_DC + 1, width), np.float32)
    mask = np.full((1, width), -np.inf, np.float32)
    for c in range(GRID_W):
        ws = min(max(c - KW // 2, 0), GRID_W - KW)
        for kc in range(ws, ws + KW):
            e = kc - c + KW - 1
            oh_l[e, c * V7X_LANES + kc] = 1.0
            oh_r[e, c * V7X_LANES + GRID_W + kc] = 1.0
            mask[0, c * V7X_LANES + kc] = 0.0
            mask[0, c * V7X_LANES + GRID_W + kc] = 0.0
    mask_last = mask.copy().reshape(GRID_W, V7X_LANES)
    mask_last[:, GRID_W:] = -np.inf
    return oh_l, oh_r, mask, mask_last.reshape(1, width)


def _split3(x):
    hi = x.astype(BF16)
    r1 = x - hi.astype(F32)
    mid = r1.astype(BF16)
    lo = (r1 - mid.astype(F32)).astype(BF16)
    return hi, mid, lo


def _bias_kernel(r1_ref, r2_ref, sel_ref, ohl_ref, ohr_ref, mask_ref, maskl_ref, o_ref):
    acc = None
    for r_ref, oh_ref in ((r1_ref, ohl_ref), (r2_ref, ohr_ref)):
        oh = oh_ref[...]
        for piece in _split3(r_ref[...]):
            t = _bdot(piece, oh)
            acc = t if acc is None else acc + t
    mask = jnp.where(sel_ref[...] > 0.5, maskl_ref[...], mask_ref[...])
    o_ref[...] = acc + mask


def _bias_tables(rpb):
    oh_l, oh_r, mask, mask_last = _bias_onehots()
    rows = N_HEADS * N_DR
    width = GRID_W * V7X_LANES
    pad = jnp.zeros((DEPTH, N_HEADS, N_DR, 1), F32)
    r1 = jnp.concatenate([rpb, pad], axis=-1).reshape(DEPTH * rows, N_DC + 1)
    nxt = jnp.concatenate([rpb[:, :, 1:], jnp.zeros((DEPTH, N_HEADS, 1, N_DC), F32)], axis=2)
    r2 = jnp.concatenate([nxt, pad], axis=-1).reshape(DEPTH * rows, N_DC + 1)
    sel = np.zeros((DEPTH, N_HEADS, N_DR, 1), np.float32)
    sel[:, :, N_DR - 1] = 1.0
    sel = jnp.asarray(sel.reshape(DEPTH * rows, 1))
    const = lambda shape: pl.BlockSpec(shape, lambda l: (0, 0))
    out = pl.pallas_call(
        _bias_kernel,
        grid=(DEPTH,),
        in_specs=[
            pl.BlockSpec((rows, N_DC + 1), lambda l: (l, 0)),
            pl.BlockSpec((rows, N_DC + 1), lambda l: (l, 0)),
            pl.BlockSpec((rows, 1), lambda l: (l, 0)),
            const((N_DC + 1, width)), const((N_DC + 1, width)),
            const((1, width)), const((1, width)),
        ],
        out_specs=pl.BlockSpec((rows, width), lambda l: (l, 0)),
        out_shape=jax.ShapeDtypeStruct((DEPTH * rows, width), F32),
        compiler_params=_params(1),
        name="bias_tables",
    )(r1, r2, sel, jnp.asarray(oh_l).astype(BF16), jnp.asarray(oh_r).astype(BF16),
      jnp.asarray(mask), jnp.asarray(mask_last))
    return out.reshape(DEPTH, N_HEADS, N_DR, GRID_W, V7X_LANES)


NORM_RC = 128


def _norm_mod_rows(x_ref, g_ref, sc_ref, sh_ref, dst_refs, rows):
    gain = g_ref[...]
    scale1 = 1.0 + sc_ref[...]
    shift = sh_ref[...]

    def body(i, carry):
        r0 = pl.multiple_of(i * NORM_RC, NORM_RC)
        x = x_ref[pl.ds(r0, NORM_RC), :]
        ms = jnp.mean(x * x, axis=-1, keepdims=True)
        y = x * lax.rsqrt(ms + EPS)
        hb = ((y * gain) * scale1 + shift).astype(BF16)
        for dst in dst_refs:
            dst[pl.ds(r0, NORM_RC), :] = hb
        return carry

    lax.fori_loop(0, rows // NORM_RC, body, 0)


IN_TN = 512
N_IN_BLK = D_IN // IN_TN
IN_Q_BLKS = D_NA // IN_TN
IN_QK_BLKS = 2 * IN_Q_BLKS
IN_QKV_BLKS = 3 * IN_Q_BLKS
KV_TILE_B = TM // SEQ


def _inproj_kernel(with_kv, x_ref, g_ref, sc_ref, sh_ref, w_ref, qkg_ref, ones_ref, cs_ref, *rest):
    if with_kv:
        h_ref, p_ref, ac_ref, as_ref, k_ref, v_ref, h_s = rest[-7:]
    else:
        h_ref, p_ref, ac_ref, as_ref, h_s = rest[-5:]
    n = pl.program_id(1)

    @pl.when(n == 0)
    def _():
        _norm_mod_rows(x_ref, g_ref, sc_ref, sh_ref, (h_s, h_ref), TM)

    y = _bdot(h_s[...], w_ref[...].astype(BF16))

    @pl.when(n < IN_QK_BLKS)
    def _():
        ss = _bdot((y * y).astype(BF16), ones_ref[...])
        yn = y * lax.rsqrt(ss * (1.0 / HEAD_DIM) + EPS) * qkg_ref[...]
        p_ref[...] = yn.astype(BF16)
        if with_kv:
            @pl.when(n >= IN_Q_BLKS)
            def _():
                k_ref[...] = yn.reshape(KV_TILE_B, SEQ, IN_TN)

    @pl.when(jnp.logical_and(n >= IN_QK_BLKS, n < IN_QKV_BLKS))
    def _():
        p_ref[...] = y.astype(BF16)
        if with_kv:
            v_ref[...] = y.reshape(KV_TILE_B, SEQ, IN_TN)

    @pl.when(n >= IN_QKV_BLKS)
    def _():
        cs = cs_ref[...]
        for gg in range(IN_TN // F_GROUP):
            a = _bdot(y[:, gg * F_GROUP:(gg + 1) * F_GROUP].astype(BF16), cs)
            ac_ref[:, gg * F_GROUP:(gg + 1) * F_GROUP] = a[:, :F_GROUP].astype(BF16)
            as_ref[:, gg * F_GROUP:(gg + 1) * F_GROUP] = a[:, F_GROUP:].astype(BF16)


def _in_projection(layer, path, x, mod, norm_g, w_in, qkg, ones_bd, cs, kv_prev=None):
    with_kv = path == CTX
    u_blk = lambda n: jnp.clip(n - IN_QKV_BLKS, 0, 1)
    in_specs = [
        _resident((TM, D_MODEL), lambda m, n: (m, 0)),
        pl.BlockSpec((None, 1, D_MODEL), lambda m, n: (layer, 0, 0)),
        _mod_spec(layer, path, 1, D_MODEL, lambda n: 0),
        _mod_spec(layer, path, 0, D_MODEL, lambda n: 0),
        pl.BlockSpec((None, D_MODEL, IN_TN), lambda m, n: (layer, 0, n)),
        pl.BlockSpec((None, None, 1, IN_TN),
                     lambda m, n: (layer, jnp.clip(n // IN_Q_BLKS, 0, 1), 0, 0)),
        pl.BlockSpec((IN_TN, IN_TN), lambda m, n: (0, 0)),
        pl.BlockSpec((F_GROUP, 2 * F_GROUP), lambda m, n: (0, 0)),
    ]
    args = [x, norm_g.reshape(DEPTH, 1, D_MODEL), mod, mod, w_in, qkg, ones_bd, cs]
    out_specs = [
        pl.BlockSpec((TM, D_MODEL), lambda m, n: (m, 0)),
        pl.BlockSpec((TM, IN_TN), lambda m, n: (m, jnp.minimum(n, IN_QKV_BLKS - 1))),
        pl.BlockSpec((TM, IN_TN), lambda m, n: (m, u_blk(n))),
        pl.BlockSpec((TM, IN_TN), lambda m, n: (m, u_blk(n))),
    ]
    out_shape = [
        jax.ShapeDtypeStruct((N_TOK, D_MODEL), BF16),
        jax.ShapeDtypeStruct((N_TOK, 3 * D_NA), BF16),
        jax.ShapeDtypeStruct((N_TOK, D_F), BF16),
        jax.ShapeDtypeStruct((N_TOK, D_F), BF16),
    ]
    aliases = {}
    if with_kv:
        kv_spec = lambda first: pl.BlockSpec(
            (KV_TILE_B, None, SEQ, IN_TN),
            lambda m, n: (m, layer, 0, jnp.clip(n - first, 0, IN_Q_BLKS - 1)))
        out_specs += [kv_spec(IN_Q_BLKS), kv_spec(IN_QK_BLKS)]
        out_shape += [jax.ShapeDtypeStruct((BATCH, DEPTH, SEQ, D_NA), F32)] * 2
        if kv_prev is not None:
            in_specs += [pl.BlockSpec(memory_space=pl.ANY)] * 2
            aliases = {len(args): 4, len(args) + 1: 5}
            args += list(kv_prev)

    def body(*refs):
        _inproj_kernel(with_kv, *refs)

    return pl.pallas_call(
        body,
        grid=(N_MT, N_IN_BLK),
        in_specs=in_specs,
        out_specs=out_specs,
        out_shape=out_shape,
        scratch_shapes=[pltpu.VMEM((TM, D_MODEL), BF16)],
        input_output_aliases=aliases,
        compiler_params=_params(2),
        name=f"in_projection_{path}_l{layer}",
    )(*args)


def _stack_heads(q):
    lane = lax.broadcasted_iota(jnp.int32, q.shape, 1)
    zero = jnp.zeros_like(q)
    return jnp.concatenate([jnp.where(lane < HEAD_DIM, q, zero),
                            jnp.where(lane >= HEAD_DIM, q, zero)], axis=0)


def _unstack_heads(o2):
    rows = o2.shape[0] // 2
    lane = lax.broadcasted_iota(jnp.int32, (rows, V7X_LANES), 1)
    return jnp.where(lane < HEAD_DIM, o2[:rows], o2[rows:])


def _qk(q2, k):
    return lax.dot_general(q2, k, (((1,), (1,)), ((), ())), preferred_element_type=F32)


def _ctx_attn_kernel(q_ref, k_ref, v_ref, o_ref):
    for p in range(N_HEADS // 2):
        cols = slice(p * V7X_LANES, (p + 1) * V7X_LANES)
        q2 = _stack_heads(q_ref[:, cols])
        s = _qk(q2, k_ref[:, cols])
        e = jnp.exp(s - jnp.max(s, axis=-1, keepdims=True))
        prob = e * (1.0 / jnp.sum(e, axis=-1, keepdims=True))
        o2 = _bdot(prob.astype(BF16), v_ref[:, cols])
        o_ref[:, cols] = _unstack_heads(o2).astype(BF16)


def _context_attention(layer, p):
    return pl.pallas_call(
        _ctx_attn_kernel,
        grid=(BATCH,),
        in_specs=[pl.BlockSpec((SEQ, D_NA), lambda b: (b, 0)),
                  pl.BlockSpec((SEQ, D_NA), lambda b: (b, 1)),
                  pl.BlockSpec((SEQ, D_NA), lambda b: (b, 2))],
        out_specs=pl.BlockSpec((SEQ, D_NA), lambda b: (b, 0)),
        out_shape=jax.ShapeDtypeStruct((N_TOK, D_NA), BF16),
        compiler_params=_params(1),
        name=f"context_attention_l{layer}",
    )(p, p, p)


NA_QROWS = 4
NA_KROWS = 12
NA_KSTART = (0, 0, 4, 4)


def _na_bias(tbl_ref, group):
    ks = NA_KSTART[group]
    lane = lax.broadcasted_iota(jnp.int32, (GRID_W, V7X_LANES), 1)
    ninf = jnp.full((GRID_W, V7X_LANES), NEG_INF, F32)
    strips = []
    for hh in range(2):
        for rr in range(NA_QROWS):
            r = NA_QROWS * group + rr
            rs = min(max(r - KH // 2, 0), ROWS - KH)
            tiles = []
            for a in range(NA_KROWS // 2):
                kk0 = ks + 2 * a
                ok0 = rs <= kk0 < rs + KH
                ok1 = rs <= kk0 + 1 < rs + KH
                dr0 = kk0 - r + KH - 1
                if ok0 and ok1:
                    t = tbl_ref[hh, dr0]
                elif ok0:
                    t = jnp.where(lane < GRID_W, tbl_ref[hh, dr0], ninf)
                elif ok1:
                    t = jnp.where(lane >= GRID_W, tbl_ref[hh, dr0], ninf)
                else:
                    t = ninf
                tiles.append(t)
            strips.append(jnp.concatenate(tiles, axis=1))
    return jnp.concatenate(strips, axis=0)


def _lat_attn_kernel(q_ref, k_ref, v_ref, kc_ref, vc_ref, tbl_ref, o_ref):
    kc = kc_ref[...].astype(BF16)
    vc = vc_ref[...].astype(BF16)
    qrows = NA_QROWS * GRID_W
    for g in range(ROWS // NA_QROWS):
        k0 = NA_KSTART[g] * GRID_W
        q2 = _stack_heads(q_ref[g * qrows:(g + 1) * qrows, :])
        s_loc = _qk(q2, k_ref[k0:k0 + NA_KROWS * GRID_W, :]) + _na_bias(tbl_ref, g)
        s_ctx = _qk(q2, kc)
        mx = jnp.maximum(jnp.max(s_loc, axis=-1, keepdims=True), jnp.max(s_ctx, axis=-1, keepdims=True))
        e_loc = jnp.exp(s_loc - mx)
        e_ctx = jnp.exp(s_ctx - mx)
        inv = 1.0 / (jnp.sum(e_loc, axis=-1, keepdims=True) + jnp.sum(e_ctx, axis=-1, keepdims=True))
        o2 = (_bdot((e_loc * inv).astype(BF16), v_ref[k0:k0 + NA_KROWS * GRID_W, :])
              + _bdot((e_ctx * inv).astype(BF16), vc))
        o_ref[g * qrows:(g + 1) * qrows, :] = _unstack_heads(o2).astype(BF16)


def _latent_attention(layer, p, cache_k, cache_v, tbl):
    n_pairs = N_HEADS // 2
    cache_spec = pl.BlockSpec((None, None, PAST_LEN, V7X_LANES), lambda b, h: (b, layer, 0, h))
    return pl.pallas_call(
        _lat_attn_kernel,
        grid=(DEC_BATCH, n_pairs),
        in_specs=[
            pl.BlockSpec((DEC_SEQ, V7X_LANES), lambda b, h: (b, h)),
            pl.BlockSpec((DEC_SEQ, V7X_LANES), lambda b, h: (b, n_pairs + h)),
            pl.BlockSpec((DEC_SEQ, V7X_LANES), lambda b, h: (b, 2 * n_pairs + h)),
            cache_spec, cache_spec,
            pl.BlockSpec((None, 2, N_DR, GRID_W, V7X_LANES), lambda b, h: (layer, h, 0, 0, 0)),
        ],
        out_specs=pl.BlockSpec((DEC_SEQ, V7X_LANES), lambda b, h: (b, h)),
        out_shape=jax.ShapeDtypeStruct((N_TOK, D_NA), BF16),
        compiler_params=_params(2),
        name=f"latent_attention_l{layer}",
    )(p, p, p, cache_k, cache_v, tbl)


def _dft_tables(n, scale):
    j = np.arange(n, dtype=np.int64)
    ang = 2.0 * np.pi * ((j[:, None] * j[None, :]) % n).astype(np.float64) / n
    return (np.cos(ang) * scale).astype(np.float32), (np.sin(ang) * scale).astype(np.float32)


def _pos_dft_kernel(ct_ref, st_ref, ac_ref, as_ref, o_ref):
    o_ref[...] = (_bdot(ct_ref[...], ac_ref[...]) - _bdot(st_ref[...], as_ref[...])).astype(BF16)


def _position_dft(layer, seq, a_cos, a_sin):
    ct, st = _dft_tables(seq, seq ** -0.5)
    blk = pl.BlockSpec((seq, D_F), lambda b: (b, 0))
    tab = pl.BlockSpec((seq, seq), lambda b: (0, 0))
    return pl.pallas_call(
        _pos_dft_kernel,
        grid=(N_TOK // seq,),
        in_specs=[tab, tab, blk, blk],
        out_specs=blk,
        out_shape=jax.ShapeDtypeStruct((N_TOK, D_F), BF16),
        compiler_params=_params(1),
        name=f"position_dft_{seq}_l{layer}",
    )(jnp.asarray(ct).astype(BF16), jnp.asarray(st).astype(BF16), a_cos, a_sin)


MIX_TC = 512
N_MIX_BLK = D_MODEL // MIX_TC
OUT_TN = 256
N_OUT_BLK = D_MODEL // OUT_TN


def _mix_kernel(h_ref, a_ref, f_ref, wgn_ref, wgf_ref, wna_ref, wf_ref, wo_ref, x_ref, g1_ref,
                o_ref, mix_s):
    s = pl.program_id(1)

    @pl.when(s < N_MIX_BLK)
    def _():
        h = h_ref[...]
        g_na = _sigmoid(_bdot(h, wgn_ref[...].astype(BF16)))
        g_fn = _sigmoid(_bdot(h, wgf_ref[...].astype(BF16)))
        na = _bdot(a_ref[...], wna_ref[...].astype(BF16))
        fn = _bdot(f_ref[...], wf_ref[...].astype(BF16))
        mix_s[s] = (g_na * na + g_fn * fn).astype(BF16)

    @pl.when(s >= N_MIX_BLK)
    def _():
        acc = None
        for c in range(N_MIX_BLK):
            t = _bdot(mix_s[c], wo_ref[c * MIX_TC:(c + 1) * MIX_TC, :].astype(BF16))
            acc = t if acc is None else acc + t
        o_ref[...] = x_ref[...] + g1_ref[...] * acc


def _token_mixing(layer, path, h, attn, f, x, mod, w_gate, w_na, w_f, w_o):
    mix_blk = lambda s: jnp.minimum(s, N_MIX_BLK - 1)
    out_blk = lambda s: jnp.clip(s - N_MIX_BLK, 0, N_OUT_BLK - 1)
    return pl.pallas_call(
        _mix_kernel,
        grid=(N_MT, N_MIX_BLK + N_OUT_BLK),
        in_specs=[
            _resident((TM, D_MODEL), lambda m, s: (m, 0)),
            _resident((TM, D_NA), lambda m, s: (m, 0)),
            _resident((TM, D_F), lambda m, s: (m, 0)),
            pl.BlockSpec((None, D_MODEL, MIX_TC), lambda m, s: (layer, 0, mix_blk(s))),
            pl.BlockSpec((None, D_MODEL, MIX_TC), lambda m, s: (layer, 0, N_MIX_BLK + mix_blk(s))),
            pl.BlockSpec((None, D_NA, MIX_TC), lambda m, s: (layer, 0, mix_blk(s))),
            pl.BlockSpec((None, D_F, MIX_TC), lambda m, s: (layer, 0, mix_blk(s))),
            pl.BlockSpec((None, D_MODEL, OUT_TN), lambda m, s: (layer, 0, out_blk(s))),
            pl.BlockSpec((TM, OUT_TN), lambda m, s: (m, out_blk(s))),
            _mod_spec(layer, path, 2, OUT_TN, out_blk),
        ],
        out_specs=pl.BlockSpec((TM, OUT_TN), lambda m, s: (m, out_blk(s))),
        out_shape=jax.ShapeDtypeStruct((N_TOK, D_MODEL), F32),
        scratch_shapes=[pltpu.VMEM((N_MIX_BLK, TM, MIX_TC), BF16)],
        compiler_params=_params(2),
        name=f"token_mixing_{path}_l{layer}",
    )(h, attn, f, w_gate, w_gate, w_na, w_f, w_o, x, mod)


NORM_TM = 512


def _norm2_kernel(x_ref, g_ref, sc_ref, sh_ref, o_ref):
    _norm_mod_rows(x_ref, g_ref, sc_ref, sh_ref, (o_ref,), NORM_TM)


def _norm_mod(layer, path, x, mod, norm_g):
    spec = lambda chunk: pl.BlockSpec((None, None, None, 1, D_MODEL),
                                      lambda m: (layer, _mod_row(path, m, NORM_TM), chunk, 0, 0))
    return pl.pallas_call(
        _norm2_kernel,
        grid=(N_TOK // NORM_TM,),
        in_specs=[pl.BlockSpec((NORM_TM, D_MODEL), lambda m: (m, 0)),
                  pl.BlockSpec((None, 1, D_MODEL), lambda m: (layer, 0, 0)),
                  spec(4), spec(3)],
        out_specs=pl.BlockSpec((NORM_TM, D_MODEL), lambda m: (m, 0)),
        out_shape=jax.ShapeDtypeStruct((N_TOK, D_MODEL), BF16),
        compiler_params=_params(1),
        name=f"norm_mod2_{path}_l{layer}",
    )(x, norm_g.reshape(DEPTH, 1, D_MODEL), mod, mod)


FF_TC = 512
N_FF_BLK = D_FF // FF_TC
DOWN_TN = 256
N_DOWN_BLK = D_MODEL // DOWN_TN


def _ffn_kernel(h_ref, wa_ref, wg_ref, wd_ref, x_ref, g2_ref, o_ref, act_s):
    s = pl.program_id(1)

    @pl.when(s < N_FF_BLK)
    def _():
        h = h_ref[...]
        a = _bdot(h, wa_ref[...].astype(BF16))
        g = _bdot(h, wg_ref[...].astype(BF16))
        act_s[s] = ((g * _sigmoid(g)) * a).astype(BF16)

    @pl.when(s >= N_FF_BLK)
    def _():
        acc = None
        for j in range(N_FF_BLK):
            t = _bdot(act_s[j], wd_ref[j * FF_TC:(j + 1) * FF_TC, :].astype(BF16))
            acc = t if acc is None else acc + t
        o_ref[...] = x_ref[...] + g2_ref[...] * acc


def _ffn(layer, path, h2, x, mod, w_gate_up, w_down):
    ff_blk = lambda s: jnp.minimum(s, N_FF_BLK - 1)
    out_blk = lambda s: jnp.clip(s - N_FF_BLK, 0, N_DOWN_BLK - 1)
    return pl.pallas_call(
        _ffn_kernel,
        grid=(N_MT, N_FF_BLK + N_DOWN_BLK),
        in_specs=[
            _resident((TM, D_MODEL), lambda m, s: (m, 0)),
            pl.BlockSpec((None, D_MODEL, FF_TC), lambda m, s: (layer, 0, ff_blk(s))),
            pl.BlockSpec((None, D_MODEL, FF_TC), lambda m, s: (layer, 0, N_FF_BLK + ff_blk(s))),
            pl.BlockSpec((None, D_FF, DOWN_TN), lambda m, s: (layer, 0, out_blk(s))),
            pl.BlockSpec((TM, DOWN_TN), lambda m, s: (m, out_blk(s))),
            _mod_spec(layer, path, 5, DOWN_TN, out_blk),
        ],
        out_specs=pl.BlockSpec((TM, DOWN_TN), lambda m, s: (m, out_blk(s))),
        out_shape=jax.ShapeDtypeStruct((N_TOK, D_MODEL), F32),
        scratch_shapes=[pltpu.VMEM((N_FF_BLK, TM, FF_TC), BF16)],
        compiler_params=_params(2),
        name=f"ffn_{path}_l{layer}",
    )(h2, w_gate_up, w_gate_up, w_down, x, mod)


def kernel(x_prompt, x_sample, cache_k, cache_v, c, c_ctx, w_mod, b_mod, norm1_g, norm2_g,
           w_in, q_norm_g, k_norm_g, rpb, w_na_proj, w_fnet_proj, w_gate, w_o, w_gate_up, w_down):
    xs = {CTX: x_prompt.reshape(N_TOK, D_MODEL), LAT: x_sample.reshape(N_TOK, D_MODEL)}
    cond8 = jnp.concatenate([c_ctx[None, :], c, jnp.zeros((8 - 1 - DEC_BATCH, D_MODEL), F32)], axis=0)
    mod = _modulation(cond8, w_mod, b_mod).reshape(DEPTH, 8, 6, 1, D_MODEL)
    tbl = _bias_tables(rpb)

    reps = IN_TN // HEAD_DIM
    qkg = jnp.stack([jnp.tile(q_norm_g * (HEAD_DIM ** -0.5), (1, reps)),
                     jnp.tile(k_norm_g, (1, reps))], axis=1).reshape(DEPTH, 2, 1, IN_TN)
    head_id = np.arange(IN_TN) // HEAD_DIM
    ones_bd = jnp.asarray((head_id[:, None] == head_id[None, :]).astype(np.float32)).astype(BF16)
    cc, sc = _dft_tables(F_GROUP, F_GROUP ** -0.5)
    cs = jnp.asarray(np.concatenate([cc, sc], axis=1)).astype(BF16)

    ck = cache_k.reshape(DEC_BATCH, DEPTH, PAST_LEN, D_NA)
    cv = cache_v.reshape(DEC_BATCH, DEPTH, PAST_LEN, D_NA)

    kv = None
    for l in range(DEPTH):
        for path in (CTX, LAT):
            x = xs[path]
            outs = _in_projection(l, path, x, mod, norm1_g, w_in, qkg, ones_bd, cs, kv_prev=kv)
            h, p, a_cos, a_sin = outs[:4]
            if path == CTX:
                kv = outs[4:]
                attn = _context_attention(l, p)
                f = _position_dft(l, SEQ, a_cos, a_sin)
            else:
                attn = _latent_attention(l, p, ck, cv, tbl)
                f = _position_dft(l, DEC_SEQ, a_cos, a_sin)
            x1 = _token_mixing(l, path, h, attn, f, x, mod, w_gate, w_na_proj, w_fnet_proj, w_o)
            h2 = _norm_mod(l, path, x1, mod, norm2_g)
            xs[path] = _ffn(l, path, h2, x1, mod, w_gate_up, w_down)

    new_k, new_v = (t.reshape(BATCH, DEPTH, SEQ, N_HEADS, HEAD_DIM) for t in kv)
    return (xs[CTX].reshape(BATCH, SEQ, D_MODEL), xs[LAT].reshape(DEC_BATCH, DEC_SEQ, D_MODEL),
            new_k, new_v)
```

```python
import numpy as np
import jax
import jax.numpy as jnp
from jax import lax
from jax.experimental import pallas as pl
from jax.experimental.pallas import tpu as pltpu

F32 = jnp.float32
BF16 = jnp.bfloat16

D_MODEL = 2048
BATCH = 16
SEQ = 256
DEPTH = 2
DEC_BATCH = 4
DEC_SEQ = 1024
PAST_LEN = 256
GRID_W = 64
ROWS = DEC_SEQ // GRID_W
N_HEADS = 16
HEAD_DIM = 64
D_NA = N_HEADS * HEAD_DIM
D_F = D_MODEL // 2
N_FGROUPS = 4
F_GROUP = D_F // N_FGROUPS
KH = 8
KW = 16
D_FF = 5632
D_IN = 3 * D_NA + D_F
EPS = 1e-6
N_DR = 2 * KH - 1
N_DC = 2 * KW - 1

N_TOK = BATCH * SEQ
assert N_TOK == DEC_BATCH * DEC_SEQ

V7X_LANES = 128
V7X_VMEM_LIMIT = 56 * 1024 * 1024

TM = 1024
N_MT = N_TOK // TM
NEG_INF = float("-inf")
CTX, LAT = "ctx", "lat"


def _params(n_axes):
    return pltpu.CompilerParams(dimension_semantics=("arbitrary",) * n_axes,
                                vmem_limit_bytes=V7X_VMEM_LIMIT)


def _mod_row(path, m, tile):
    return 0 if path == CTX else 1 + m // (DEC_SEQ // tile)


def _mod_spec(layer, path, chunk, width, col_map):
    return pl.BlockSpec((None, None, None, 1, width),
                        lambda m, s: (layer, _mod_row(path, m, TM), chunk, 0, col_map(s)))


def _sigmoid(z):
    return 1.0 / (1.0 + jnp.exp(-z))


def _bdot(a, b):
    return jnp.dot(a, b, preferred_element_type=F32)


MOD_TN = 1024


def _mod_kernel(cond_ref, w_ref, b_ref, o_ref):
    cnd = cond_ref[...]
    s = (cnd * _sigmoid(cnd)).astype(BF16)
    o_ref[...] = _bdot(s, w_ref[...].astype(BF16)) + b_ref[...]


def _modulation(cond8, w_mod, b_mod):
    n_blk = 6 * D_MODEL // MOD_TN
    return pl.pallas_call(
        _mod_kernel,
        grid=(DEPTH, n_blk),
        in_specs=[
            pl.BlockSpec((8, D_MODEL), lambda l, n: (0, 0)),
            pl.BlockSpec((None, D_MODEL, MOD_TN), lambda l, n: (l, 0, n)),
            pl.BlockSpec((None, 1, MOD_TN), lambda l, n: (l, 0, n)),
        ],
        out_specs=pl.BlockSpec((None, 8, MOD_TN), lambda l, n: (l, 0, n)),
        out_shape=jax.ShapeDtypeStruct((DEPTH, 8, 6 * D_MODEL), F32),
        compiler_params=_params(2),
        name="modulation",
    )(cond8, w_mod, b_mod.reshape(DEPTH, 1, 6 * D_MODEL))


def _bias_onehots():
    width = GRID_W * V7X_LANES
    oh_l = np.zeros((N_DC + 1, width), np.float32)
    oh_r = np.zeros((N_DC + 1, width), np.float32)
    mask = np.full((1, width), -np.inf, np.float32)
    for c in range(GRID_W):
        ws = min(max(c - KW // 2, 0), GRID_W - KW)
        for kc in range(ws, ws + KW):
            e = kc - c + KW - 1
            oh_l[e, c * V7X_LANES + kc] = 1.0
            oh_r[e, c * V7X_LANES + GRID_W + kc] = 1.0
            mask[0, c * V7X_LANES + kc] = 0.0
            mask[0, c * V7X_LANES + GRID_W + kc] = 0.0
    mask_last = mask.copy().reshape(GRID_W, V7X_LANES)
    mask_last[:, GRID_W:] = -np.inf
    return oh_l, oh_r, mask, mask_last.reshape(1, width)


def _split3(x):
    hi = x.astype(BF16)
    r1 = x - hi.astype(F32)
    mid = r1.astype(BF16)
    lo = (r1 - mid.astype(F32)).astype(BF16)
    return hi, mid, lo


def _bias_kernel(r1_ref, r2_ref, sel_ref, ohl_ref, ohr_ref, mask_ref, maskl_ref, o_ref):
    acc = None
    for r_ref, oh_ref in ((r1_ref, ohl_ref), (r2_ref, ohr_ref)):
        oh = oh_ref[...]
        for piece in _split3(r_ref[...]):
            t = _bdot(piece, oh)
            acc = t if acc is None else acc + t
    mask = jnp.where(sel_ref[...] > 0.5, maskl_ref[...], mask_ref[...])
    o_ref[...] = acc + mask


def _bias_tables(rpb):
    oh_l, oh_r, mask, mask_last = _bias_onehots()
    rows = N_HEADS * N_DR
    width = GRID_W * V7X_LANES
    pad = jnp.zeros((DEPTH, N_HEADS, N_DR, 1), F32)
    r1 = jnp.concatenate([rpb, pad], axis=-1).reshape(DEPTH * rows, N_DC + 1)
    nxt = jnp.concatenate([rpb[:, :, 1:], jnp.zeros((DEPTH, N_HEADS, 1, N_DC), F32)], axis=2)
    r2 = jnp.concatenate([nxt, pad], axis=-1).reshape(DEPTH * rows, N_DC + 1)
    sel = np.zeros((DEPTH, N_HEADS, N_DR, 1), np.float32)
    sel[:, :, N_DR - 1] = 1.0
    sel = jnp.asarray(sel.reshape(DEPTH * rows, 1))
    const = lambda shape: pl.BlockSpec(shape, lambda l: (0, 0))
    out = pl.pallas_call(
        _bias_kernel,
        grid=(DEPTH,),
        in_specs=[
            pl.BlockSpec((rows, N_DC + 1), lambda l: (l, 0)),
            pl.BlockSpec((rows, N_DC + 1), lambda l: (l, 0)),
            pl.BlockSpec((rows, 1), lambda l: (l, 0)),
            const((N_DC + 1, width)), const((N_DC + 1, width)),
            const((1, width)), const((1, width)),
        ],
        out_specs=pl.BlockSpec((rows, width), lambda l: (l, 0)),
        out_shape=jax.ShapeDtypeStruct((DEPTH * rows, width), F32),
        compiler_params=_params(1),
        name="bias_tables",
    )(r1, r2, sel, jnp.asarray(oh_l).astype(BF16), jnp.asarray(oh_r).astype(BF16),
      jnp.asarray(mask), jnp.asarray(mask_last))
    return out.reshape(DEPTH, N_HEADS, N_DR, GRID_W, V7X_LANES)


NORM_RC = 128


def _norm_mod_rows(x_ref, g_ref, sc_ref, sh_ref, dst_ref, inv_ref, rows):
    def stats(i, carry):
        r0 = pl.multiple_of(i * NORM_RC, NORM_RC)
        x = x_ref[pl.ds(r0, NORM_RC), :]
        inv_ref[pl.ds(r0, NORM_RC), :] = lax.rsqrt(jnp.mean(x * x, axis=-1, keepdims=True) + EPS)
        return carry

    lax.fori_loop(0, rows // NORM_RC, stats, 0)
    gain = g_ref[...]
    scale1 = 1.0 + sc_ref[...]
    shift = sh_ref[...]

    def apply(i, carry):
        r0 = pl.multiple_of(i * NORM_RC, NORM_RC)
        y = x_ref[pl.ds(r0, NORM_RC), :] * inv_ref[pl.ds(r0, NORM_RC), :]
        dst_ref[pl.ds(r0, NORM_RC), :] = ((y * gain) * scale1 + shift).astype(BF16)
        return carry

    lax.fori_loop(0, rows // NORM_RC, apply, 0)


IN_TN = 512
N_IN_BLK = D_IN // IN_TN
IN_Q_BLKS = D_NA // IN_TN
IN_QK_BLKS = 2 * IN_Q_BLKS
IN_QKV_BLKS = 3 * IN_Q_BLKS
KV_TILE_B = TM // SEQ


def _inproj_kernel(with_kv, x_ref, g_ref, sc_ref, sh_ref, w_ref, qkg_ref, ones_ref, cs_ref, *rest):
    if with_kv:
        h_ref, p_ref, ac_ref, as_ref, k_ref, v_ref, inv_s = rest[-7:]
    else:
        h_ref, p_ref, ac_ref, as_ref, inv_s = rest[-5:]
    n = pl.program_id(1)

    @pl.when(n == 0)
    def _():
        _norm_mod_rows(x_ref, g_ref, sc_ref, sh_ref, h_ref, inv_s, TM)

    y = _bdot(h_ref[...], w_ref[...].astype(BF16))

    @pl.when(n < IN_QK_BLKS)
    def _():
        ss = _bdot((y * y).astype(BF16), ones_ref[...])
        yn = y * lax.rsqrt(ss * (1.0 / HEAD_DIM) + EPS) * qkg_ref[...]
        p_ref[...] = yn.astype(BF16)
        if with_kv:
            @pl.when(n >= IN_Q_BLKS)
            def _():
                k_ref[...] = yn.reshape(KV_TILE_B, SEQ, IN_TN)

    @pl.when(jnp.logical_and(n >= IN_QK_BLKS, n < IN_QKV_BLKS))
    def _():
        p_ref[...] = y.astype(BF16)
        if with_kv:
            v_ref[...] = y.reshape(KV_TILE_B, SEQ, IN_TN)

    @pl.when(n >= IN_QKV_BLKS)
    def _():
        cs = cs_ref[...]
        for gg in range(IN_TN // F_GROUP):
            a = _bdot(y[:, gg * F_GROUP:(gg + 1) * F_GROUP].astype(BF16), cs)
            ac_ref[:, gg * F_GROUP:(gg + 1) * F_GROUP] = a[:, :F_GROUP].astype(BF16)
            as_ref[:, gg * F_GROUP:(gg + 1) * F_GROUP] = a[:, F_GROUP:].astype(BF16)


def _in_projection(layer, path, x, mod, norm_g, w_in, qkg, ones_bd, cs, kv_prev=None):
    with_kv = path == CTX
    u_blk = lambda n: jnp.clip(n - IN_QKV_BLKS, 0, 1)
    in_specs = [
        pl.BlockSpec((TM, D_MODEL), lambda m, n: (m, 0)),
        pl.BlockSpec((None, 1, D_MODEL), lambda m, n: (layer, 0, 0)),
        _mod_spec(layer, path, 1, D_MODEL, lambda n: 0),
        _mod_spec(layer, path, 0, D_MODEL, lambda n: 0),
        pl.BlockSpec((None, D_MODEL, IN_TN), lambda m, n: (layer, 0, n)),
        pl.BlockSpec((None, None, 1, IN_TN),
                     lambda m, n: (layer, jnp.clip(n // IN_Q_BLKS, 0, 1), 0, 0)),
        pl.BlockSpec((IN_TN, IN_TN), lambda m, n: (0, 0)),
        pl.BlockSpec((F_GROUP, 2 * F_GROUP), lambda m, n: (0, 0)),
    ]
    args = [x, norm_g.reshape(DEPTH, 1, D_MODEL), mod, mod, w_in, qkg, ones_bd, cs]
    out_specs = [
        pl.BlockSpec((TM, D_MODEL), lambda m, n: (m, 0)),
        pl.BlockSpec((TM, IN_TN), lambda m, n: (m, jnp.minimum(n, IN_QKV_BLKS - 1))),
        pl.BlockSpec((TM, IN_TN), lambda m, n: (m, u_blk(n))),
        pl.BlockSpec((TM, IN_TN), lambda m, n: (m, u_blk(n))),
    ]
    out_shape = [
        jax.ShapeDtypeStruct((N_TOK, D_MODEL), BF16),
        jax.ShapeDtypeStruct((N_TOK, 3 * D_NA), BF16),
        jax.ShapeDtypeStruct((N_TOK, D_F), BF16),
        jax.ShapeDtypeStruct((N_TOK, D_F), BF16),
    ]
    aliases = {}
    if with_kv:
        kv_spec = lambda first: pl.BlockSpec(
            (KV_TILE_B, None, SEQ, IN_TN),
            lambda m, n: (m, layer, 0, jnp.clip(n - first, 0, IN_Q_BLKS - 1)))
        out_specs += [kv_spec(IN_Q_BLKS), kv_spec(IN_QK_BLKS)]
        out_shape += [jax.ShapeDtypeStruct((BATCH, DEPTH, SEQ, D_NA), F32)] * 2
        if kv_prev is not None:
            in_specs += [pl.BlockSpec(memory_space=pl.ANY)] * 2
            aliases = {len(args): 4, len(args) + 1: 5}
            args += list(kv_prev)

    def body(*refs):
        _inproj_kernel(with_kv, *refs)

    return pl.pallas_call(
        body,
        grid=(N_MT, N_IN_BLK),
        in_specs=in_specs,
        out_specs=out_specs,
        out_shape=out_shape,
        scratch_shapes=[pltpu.VMEM((TM, 1), F32)],
        input_output_aliases=aliases,
        compiler_params=_params(2),
        name=f"in_projection_{path}_l{layer}",
    )(*args)


def _stack_heads(q):
    lane = lax.broadcasted_iota(jnp.int32, q.shape, 1)
    zero = jnp.zeros_like(q)
    return jnp.concatenate([jnp.where(lane < HEAD_DIM, q, zero),
                            jnp.where(lane >= HEAD_DIM, q, zero)], axis=0)


def _unstack_heads(o2):
    rows = o2.shape[0] // 2
    lane = lax.broadcasted_iota(jnp.int32, (rows, V7X_LANES), 1)
    return jnp.where(lane < HEAD_DIM, o2[:rows], o2[rows:])


def _qk(q2, k):
    return lax.dot_general(q2, k, (((1,), (1,)), ((), ())), preferred_element_type=F32)


def _softmax(scores):
    mx = scores[0].max(axis=-1, keepdims=True)
    for s in scores[1:]:
        mx = jnp.maximum(mx, s.max(axis=-1, keepdims=True))
    exps = [jnp.exp(s - mx) for s in scores]
    den = exps[0].sum(axis=-1, keepdims=True)
    for e in exps[1:]:
        den = den + e.sum(axis=-1, keepdims=True)
    inv = 1.0 / den
    return [(e * inv).astype(BF16) for e in exps]


def _ctx_attn_kernel(q_ref, k_ref, v_ref, o_ref):
    for p in range(N_HEADS // 2):
        cols = slice(p * V7X_LANES, (p + 1) * V7X_LANES)
        q2 = _stack_heads(q_ref[:, cols])
        prob, = _softmax([_qk(q2, k_ref[:, cols])])
        o2 = _bdot(prob, v_ref[:, cols])
        o_ref[:, cols] = _unstack_heads(o2).astype(BF16)


def _context_attention(layer, p):
    return pl.pallas_call(
        _ctx_attn_kernel,
        grid=(BATCH,),
        in_specs=[pl.BlockSpec((SEQ, D_NA), lambda b: (b, 0)),
                  pl.BlockSpec((SEQ, D_NA), lambda b: (b, 1)),
                  pl.BlockSpec((SEQ, D_NA), lambda b: (b, 2))],
        out_specs=pl.BlockSpec((SEQ, D_NA), lambda b: (b, 0)),
        out_shape=jax.ShapeDtypeStruct((N_TOK, D_NA), BF16),
        compiler_params=_params(1),
        name=f"context_attention_l{layer}",
    )(p, p, p)


NA_QROWS = 4
NA_KROWS = 12
NA_KSTART = (0, 0, 4, 4)


def _na_bias_strip(tbl_ref, group, strip):
    hh, rr = divmod(strip, NA_QROWS)
    ks = NA_KSTART[group]
    lane = lax.broadcasted_iota(jnp.int32, (GRID_W, V7X_LANES), 1)
    ninf = jnp.full((GRID_W, V7X_LANES), NEG_INF, F32)
    r = NA_QROWS * group + rr
    rs = min(max(r - KH // 2, 0), ROWS - KH)
    tiles = []
    for a in range(NA_KROWS // 2):
        kk0 = ks + 2 * a
        ok0 = rs <= kk0 < rs + KH
        ok1 = rs <= kk0 + 1 < rs + KH
        dr0 = kk0 - r + KH - 1
        if ok0 and ok1:
            t = tbl_ref[hh, dr0]
        elif ok0:
            t = jnp.where(lane < GRID_W, tbl_ref[hh, dr0], ninf)
        elif ok1:
            t = jnp.where(lane >= GRID_W, tbl_ref[hh, dr0], ninf)
        else:
            t = ninf
        tiles.append(t)
    return jnp.concatenate(tiles, axis=1)


def _lat_attn_kernel(q_ref, k_ref, v_ref, kc_ref, vc_ref, tbl_ref, o_ref):
    kc = kc_ref[...].astype(BF16)
    vc = vc_ref[...].astype(BF16)
    qrows = NA_QROWS * GRID_W
    for g in range(ROWS // NA_QROWS):
        k0 = NA_KSTART[g] * GRID_W
        q2 = _stack_heads(q_ref[g * qrows:(g + 1) * qrows, :])
        bias = jnp.concatenate([_na_bias_strip(tbl_ref, g, t) for t in range(2 * NA_QROWS)], axis=0)
        s_loc = _qk(q2, k_ref[k0:k0 + NA_KROWS * GRID_W, :]) + bias
        p_loc, p_ctx = _softmax([s_loc, _qk(q2, kc)])
        o2 = _bdot(p_loc, v_ref[k0:k0 + NA_KROWS * GRID_W, :]) + _bdot(p_ctx, vc)
        o_ref[g * qrows:(g + 1) * qrows, :] = _unstack_heads(o2).astype(BF16)


def _latent_attention(layer, p, cache_k, cache_v, tbl):
    n_pairs = N_HEADS // 2
    cache_spec = pl.BlockSpec((None, None, PAST_LEN, V7X_LANES), lambda b, h: (b, layer, 0, h))
    return pl.pallas_call(
        _lat_attn_kernel,
        grid=(DEC_BATCH, n_pairs),
        in_specs=[
            pl.BlockSpec((DEC_SEQ, V7X_LANES), lambda b, h: (b, h)),
            pl.BlockSpec((DEC_SEQ, V7X_LANES), lambda b, h: (b, n_pairs + h)),
            pl.BlockSpec((DEC_SEQ, V7X_LANES), lambda b, h: (b, 2 * n_pairs + h)),
            cache_spec, cache_spec,
            pl.BlockSpec((None, 2, N_DR, GRID_W, V7X_LANES), lambda b, h: (layer, h, 0, 0, 0)),
        ],
        out_specs=pl.BlockSpec((DEC_SEQ, V7X_LANES), lambda b, h: (b, h)),
        out_shape=jax.ShapeDtypeStruct((N_TOK, D_NA), BF16),
        compiler_params=_params(2),
        name=f"latent_attention_l{layer}",
    )(p, p, p, cache_k, cache_v, tbl)


def _dft_tables(n, scale):
    j = np.arange(n, dtype=np.int64)
    ang = 2.0 * np.pi * ((j[:, None] * j[None, :]) % n).astype(np.float64) / n
    return (np.cos(ang) * scale).astype(np.float32), (np.sin(ang) * scale).astype(np.float32)


DFT_ROWS = 1024


def _pos_dft_kernel(seq, ct_ref, st_ref, ac_ref, as_ref, o_ref):
    ct = ct_ref[...]
    st = st_ref[...]
    for i in range(DFT_ROWS // seq):
        rows = slice(i * seq, (i + 1) * seq)
        o_ref[rows, :] = (_bdot(ct, ac_ref[rows, :]) - _bdot(st, as_ref[rows, :])).astype(BF16)


def _position_dft(layer, seq, a_cos, a_sin):
    ct, st = _dft_tables(seq, seq ** -0.5)
    blk = pl.BlockSpec((DFT_ROWS, D_F), lambda b: (b, 0))
    tab = pl.BlockSpec((seq, seq), lambda b: (0, 0))

    def body(*refs):
        _pos_dft_kernel(seq, *refs)

    return pl.pallas_call(
        body,
        grid=(N_TOK // DFT_ROWS,),
        in_specs=[tab, tab, blk, blk],
        out_specs=blk,
        out_shape=jax.ShapeDtypeStruct((N_TOK, D_F), BF16),
        compiler_params=_params(1),
        name=f"position_dft_{seq}_l{layer}",
    )(jnp.asarray(ct).astype(BF16), jnp.asarray(st).astype(BF16), a_cos, a_sin)


MIX_TC = 256
N_MIX_BLK = D_MODEL // MIX_TC
OUT_TN = 256
N_OUT_BLK = D_MODEL // OUT_TN


def _mix_kernel(h_ref, a_ref, f_ref, wgn_ref, wgf_ref, wna_ref, wf_ref, wo_ref, x_ref, g1_ref,
                o_ref, mix_s):
    s = pl.program_id(1)

    @pl.when(s < N_MIX_BLK)
    def _():
        h = h_ref[...]
        g_na = _sigmoid(_bdot(h, wgn_ref[...].astype(BF16)))
        g_fn = _sigmoid(_bdot(h, wgf_ref[...].astype(BF16)))
        na = _bdot(a_ref[...], wna_ref[...].astype(BF16))
        fn = _bdot(f_ref[...], wf_ref[...].astype(BF16))
        mix_s[s] = (g_na * na + g_fn * fn).astype(BF16)

    @pl.when(s >= N_MIX_BLK)
    def _():
        acc = None
        for c in range(N_MIX_BLK):
            t = _bdot(mix_s[c], wo_ref[c * MIX_TC:(c + 1) * MIX_TC, :].astype(BF16))
            acc = t if acc is None else acc + t
        o_ref[...] = x_ref[...] + g1_ref[...] * acc


def _token_mixing(layer, path, h, attn, f, x, mod, w_gate, w_na, w_f, w_o):
    mix_blk = lambda s: jnp.minimum(s, N_MIX_BLK - 1)
    out_blk = lambda s: jnp.clip(s - N_MIX_BLK, 0, N_OUT_BLK - 1)
    return pl.pallas_call(
        _mix_kernel,
        grid=(N_MT, N_MIX_BLK + N_OUT_BLK),
        in_specs=[
            pl.BlockSpec((TM, D_MODEL), lambda m, s: (m, 0)),
            pl.BlockSpec((TM, D_NA), lambda m, s: (m, 0)),
            pl.BlockSpec((TM, D_F), lambda m, s: (m, 0)),
            pl.BlockSpec((None, D_MODEL, MIX_TC), lambda m, s: (layer, 0, mix_blk(s))),
            pl.BlockSpec((None, D_MODEL, MIX_TC), lambda m, s: (layer, 0, N_MIX_BLK + mix_blk(s))),
            pl.BlockSpec((None, D_NA, MIX_TC), lambda m, s: (layer, 0, mix_blk(s))),
            pl.BlockSpec((None, D_F, MIX_TC), lambda m, s: (layer, 0, mix_blk(s))),
            pl.BlockSpec((None, D_MODEL, OUT_TN), lambda m, s: (layer, 0, out_blk(s))),
            pl.BlockSpec((TM, OUT_TN), lambda m, s: (m, out_blk(s))),
            _mod_spec(layer, path, 2, OUT_TN, out_blk),
        ],
        out_specs=pl.BlockSpec((TM, OUT_TN), lambda m, s: (m, out_blk(s))),
        out_shape=jax.ShapeDtypeStruct((N_TOK, D_MODEL), F32),
        scratch_shapes=[pltpu.VMEM((N_MIX_BLK, TM, MIX_TC), BF16)],
        compiler_params=_params(2),
        name=f"token_mixing_{path}_l{layer}",
    )(h, attn, f, w_gate, w_gate, w_na, w_f, w_o, x, mod)


NORM_TM = 512


def _norm2_kernel(x_ref, g_ref, sc_ref, sh_ref, o_ref, inv_s):
    _norm_mod_rows(x_ref, g_ref, sc_ref, sh_ref, o_ref, inv_s, NORM_TM)


def _norm_mod(layer, path, x, mod, norm_g):
    spec = lambda chunk: pl.BlockSpec((None, None, None, 1, D_MODEL),
                                      lambda m: (layer, _mod_row(path, m, NORM_TM), chunk, 0, 0))
    return pl.pallas_call(
        _norm2_kernel,
        grid=(N_TOK // NORM_TM,),
        in_specs=[pl.BlockSpec((NORM_TM, D_MODEL), lambda m: (m, 0)),
                  pl.BlockSpec((None, 1, D_MODEL), lambda m: (layer, 0, 0)),
                  spec(4), spec(3)],
        out_specs=pl.BlockSpec((NORM_TM, D_MODEL), lambda m: (m, 0)),
        out_shape=jax.ShapeDtypeStruct((N_TOK, D_MODEL), BF16),
        scratch_shapes=[pltpu.VMEM((NORM_TM, 1), F32)],
        compiler_params=_params(1),
        name=f"norm_mod2_{path}_l{layer}",
    )(x, norm_g.reshape(DEPTH, 1, D_MODEL), mod, mod)


FF_TC = 256
N_FF_BLK = D_FF // FF_TC
DOWN_TN = 256
N_DOWN_BLK = D_MODEL // DOWN_TN


def _ffn_kernel(h_ref, wa_ref, wg_ref, wd_ref, x_ref, g2_ref, o_ref, act_s):
    s = pl.program_id(1)

    @pl.when(s < N_FF_BLK)
    def _():
        h = h_ref[...]
        a = _bdot(h, wa_ref[...].astype(BF16))
        g = _bdot(h, wg_ref[...].astype(BF16))
        act_s[s] = ((g * _sigmoid(g)) * a).astype(BF16)

    @pl.when(s >= N_FF_BLK)
    def _():
        acc = None
        for j in range(N_FF_BLK):
            t = _bdot(act_s[j], wd_ref[j * FF_TC:(j + 1) * FF_TC, :].astype(BF16))
            acc = t if acc is None else acc + t
        o_ref[...] = x_ref[...] + g2_ref[...] * acc


def _ffn(layer, path, h2, x, mod, w_gate_up, w_down):
    ff_blk = lambda s: jnp.minimum(s, N_FF_BLK - 1)
    out_blk = lambda s: jnp.clip(s - N_FF_BLK, 0, N_DOWN_BLK - 1)
    return pl.pallas_call(
        _ffn_kernel,
        grid=(N_MT, N_FF_BLK + N_DOWN_BLK),
        in_specs=[
            pl.BlockSpec((TM, D_MODEL), lambda m, s: (m, 0)),
            pl.BlockSpec((None, D_MODEL, FF_TC), lambda m, s: (layer, 0, ff_blk(s))),
            pl.BlockSpec((None, D_MODEL, FF_TC), lambda m, s: (layer, 0, N_FF_BLK + ff_blk(s))),
            pl.BlockSpec((None, D_FF, DOWN_TN), lambda m, s: (layer, 0, out_blk(s))),
            pl.BlockSpec((TM, DOWN_TN), lambda m, s: (m, out_blk(s))),
            _mod_spec(layer, path, 5, DOWN_TN, out_blk),
        ],
        out_specs=pl.BlockSpec((TM, DOWN_TN), lambda m, s: (m, out_blk(s))),
        out_shape=jax.ShapeDtypeStruct((N_TOK, D_MODEL), F32),
        scratch_shapes=[pltpu.VMEM((N_FF_BLK, TM, FF_TC), BF16)],
        compiler_params=_params(2),
        name=f"ffn_{path}_l{layer}",
    )(h2, w_gate_up, w_gate_up, w_down, x, mod)


def kernel(x_prompt, x_sample, cache_k, cache_v, c, c_ctx, w_mod, b_mod, norm1_g, norm2_g,
           w_in, q_norm_g, k_norm_g, rpb, w_na_proj, w_fnet_proj, w_gate, w_o, w_gate_up, w_down):
    xs = {CTX: x_prompt.reshape(N_TOK, D_MODEL), LAT: x_sample.reshape(N_TOK, D_MODEL)}
    cond8 = jnp.concatenate([c_ctx[None, :], c, jnp.zeros((8 - 1 - DEC_BATCH, D_MODEL), F32)], axis=0)
    mod = _modulation(cond8, w_mod, b_mod).reshape(DEPTH, 8, 6, 1, D_MODEL)
    tbl = _bias_tables(rpb)

    reps = IN_TN // HEAD_DIM
    qkg = jnp.stack([jnp.tile(q_norm_g * (HEAD_DIM ** -0.5), (1, reps)),
                     jnp.tile(k_norm_g, (1, reps))], axis=1).reshape(DEPTH, 2, 1, IN_TN)
    head_id = np.arange(IN_TN) // HEAD_DIM
    ones_bd = jnp.asarray((head_id[:, None] == head_id[None, :]).astype(np.float32)).astype(BF16)
    cc, sc = _dft_tables(F_GROUP, F_GROUP ** -0.5)
    cs = jnp.asarray(np.concatenate([cc, sc], axis=1)).astype(BF16)

    ck = cache_k.reshape(DEC_BATCH, DEPTH, PAST_LEN, D_NA)
    cv = cache_v.reshape(DEC_BATCH, DEPTH, PAST_LEN, D_NA)

    kv = None
    for l in range(DEPTH):
        for path in (CTX, LAT):
            x = xs[path]
            outs = _in_projection(l, path, x, mod, norm1_g, w_in, qkg, ones_bd, cs, kv_prev=kv)
            h, p, a_cos, a_sin = outs[:4]
            if path == CTX:
                kv = outs[4:]
                attn = _context_attention(l, p)
                f = _position_dft(l, SEQ, a_cos, a_sin)
            else:
                attn = _latent_attention(l, p, ck, cv, tbl)
                f = _position_dft(l, DEC_SEQ, a_cos, a_sin)
            x1 = _token_mixing(l, path, h, attn, f, x, mod, w_gate, w_na_proj, w_fnet_proj, w_o)
            h2 = _norm_mod(l, path, x1, mod, norm2_g)
            xs[path] = _ffn(l, path, h2, x1, mod, w_gate_up, w_down)

    new_k, new_v = (t.reshape(BATCH, DEPTH, SEQ, N_HEADS, HEAD_DIM) for t in kv)
    return (xs[CTX].reshape(BATCH, SEQ, D_MODEL), xs[LAT].reshape(DEC_BATCH, DEC_SEQ, D_MODEL),
            new_k, new_v)
```

```python
import numpy as np
import jax
import jax.numpy as jnp
from jax import lax
from jax.experimental import pallas as pl
from jax.experimental.pallas import tpu as pltpu

F32 = jnp.float32
BF16 = jnp.bfloat16

D_MODEL = 2048
BATCH = 16
SEQ = 256
DEPTH = 2
DEC_BATCH = 4
DEC_SEQ = 1024
PAST_LEN = 256
GRID_W = 64
ROWS = DEC_SEQ // GRID_W
N_HEADS = 16
HEAD_DIM = 64
D_NA = N_HEADS * HEAD_DIM
D_F = D_MODEL // 2
N_FGROUPS = 4
F_GROUP = D_F // N_FGROUPS
KH = 8
KW = 16
D_FF = 5632
D_IN = 3 * D_NA + D_F
EPS = 1e-6
N_DR = 2 * KH - 1
N_DC = 2 * KW - 1

N_TOK = BATCH * SEQ
assert N_TOK == DEC_BATCH * DEC_SEQ

V7X_LANES = 128
V7X_VMEM_LIMIT = 56 * 1024 * 1024

TM = 1024
N_MT = N_TOK // TM
NEG_INF = float("-inf")
CTX, LAT = "ctx", "lat"


def _params(n_axes):
    return pltpu.CompilerParams(dimension_semantics=("arbitrary",) * n_axes,
                                vmem_limit_bytes=V7X_VMEM_LIMIT)


def _mod_row(path, m, tile):
    return 0 if path == CTX else 1 + m // (DEC_SEQ // tile)


def _mod_spec(layer, path, chunk, width, col_map):
    return pl.BlockSpec((None, None, None, 1, width),
                        lambda m, s: (layer, _mod_row(path, m, TM), chunk, 0, col_map(s)))


def _sigmoid(z):
    return 1.0 / (1.0 + jnp.exp(-z))


def _bdot(a, b):
    return jnp.dot(a, b, preferred_element_type=F32)


MOD_TN = 1024


def _mod_kernel(cond_ref, w_ref, b_ref, o_ref):
    cnd = cond_ref[...]
    s = (cnd * _sigmoid(cnd)).astype(BF16)
    o_ref[...] = _bdot(s, w_ref[...].astype(BF16)) + b_ref[...]


def _modulation(cond8, w_mod, b_mod):
    n_blk = 6 * D_MODEL // MOD_TN
    return pl.pallas_call(
        _mod_kernel,
        grid=(DEPTH, n_blk),
        in_specs=[
            pl.BlockSpec((8, D_MODEL), lambda l, n: (0, 0)),
            pl.BlockSpec((None, D_MODEL, MOD_TN), lambda l, n: (l, 0, n)),
            pl.BlockSpec((None, 1, MOD_TN), lambda l, n: (l, 0, n)),
        ],
        out_specs=pl.BlockSpec((None, 8, MOD_TN), lambda l, n: (l, 0, n)),
        out_shape=jax.ShapeDtypeStruct((DEPTH, 8, 6 * D_MODEL), F32),
        compiler_params=_params(2),
        name="modulation",
    )(cond8, w_mod, b_mod.reshape(DEPTH, 1, 6 * D_MODEL))


def _bias_onehots():
    width = GRID_W * V7X_LANES
    oh_l = np.zeros((N_DC + 1, width), np.float32)
    oh_r = np.zeros((N_DC + 1, width), np.float32)
    mask = np.full((1, width), -np.inf, np.float32)
    for c in range(GRID_W):
        ws = min(max(c - KW // 2, 0), GRID_W - KW)
        for kc in range(ws, ws + KW):
            e = kc - c + KW - 1
            oh_l[e, c * V7X_LANES + kc] = 1.0
            oh_r[e, c * V7X_LANES + GRID_W + kc] = 1.0
            mask[0, c * V7X_LANES + kc] = 0.0
            mask[0, c * V7X_LANES + GRID_W + kc] = 0.0
    mask_last = mask.copy().reshape(GRID_W, V7X_LANES)
    mask_last[:, GRID_W:] = -np.inf
    return oh_l, oh_r, mask, mask_last.reshape(1, width)


def _split3(x):
    hi = x.astype(BF16)
    r1 = x - hi.astype(F32)
    mid = r1.astype(BF16)
    lo = (r1 - mid.astype(F32)).astype(BF16)
    return hi, mid, lo


def _bias_kernel(r1_ref, r2_ref, sel_ref, ohl_ref, ohr_ref, mask_ref, maskl_ref, o_ref):
    acc = None
    for r_ref, oh_ref in ((r1_ref, ohl_ref), (r2_ref, ohr_ref)):
        oh = oh_ref[...]
        for piece in _split3(r_ref[...]):
            t = _bdot(piece, oh)
            acc = t if acc is None else acc + t
    mask = jnp.where(sel_ref[...] > 0.5, maskl_ref[...], mask_ref[...])
    o_ref[...] = acc + mask


def _bias_tables(rpb):
    oh_l, oh_r, mask, mask_last = _bias_onehots()
    rows = N_HEADS * N_DR
    width = GRID_W * V7X_LANES
    pad = jnp.zeros((DEPTH, N_HEADS, N_DR, 1), F32)
    r1 = jnp.concatenate([rpb, pad], axis=-1).reshape(DEPTH * rows, N_DC + 1)
    nxt = jnp.concatenate([rpb[:, :, 1:], jnp.zeros((DEPTH, N_HEADS, 1, N_DC), F32)], axis=2)
    r2 = jnp.concatenate([nxt, pad], axis=-1).reshape(DEPTH * rows, N_DC + 1)
    sel = np.zeros((DEPTH, N_HEADS, N_DR, 1), np.float32)
    sel[:, :, N_DR - 1] = 1.0
    sel = jnp.asarray(sel.reshape(DEPTH * rows, 1))
    const = lambda shape: pl.BlockSpec(shape, lambda l: (0, 0))
    out = pl.pallas_call(
        _bias_kernel,
        grid=(DEPTH,),
        in_specs=[
            pl.BlockSpec((rows, N_DC + 1), lambda l: (l, 0)),
            pl.BlockSpec((rows, N_DC + 1), lambda l: (l, 0)),
            pl.BlockSpec((rows, 1), lambda l: (l, 0)),
            const((N_DC + 1, width)), const((N_DC + 1, width)),
            const((1, width)), const((1, width)),
        ],
        out_specs=pl.BlockSpec((rows, width), lambda l: (l, 0)),
        out_shape=jax.ShapeDtypeStruct((DEPTH * rows, width), F32),
        compiler_params=_params(1),
        name="bias_tables",
    )(r1, r2, sel, jnp.asarray(oh_l).astype(BF16), jnp.asarray(oh_r).astype(BF16),
      jnp.asarray(mask), jnp.asarray(mask_last))
    return out.reshape(DEPTH, N_HEADS, N_DR, GRID_W, V7X_LANES)


NORM_RC = 128


def _norm_mod_rows(x_ref, g_ref, sc_ref, sh_ref, dst_ref, inv_ref, rows):
    def stats(i, carry):
        r0 = pl.multiple_of(i * NORM_RC, NORM_RC)
        x = x_ref[pl.ds(r0, NORM_RC), :]
        inv_ref[pl.ds(r0, NORM_RC), :] = lax.rsqrt(jnp.mean(x * x, axis=-1, keepdims=True) + EPS)
        return carry

    lax.fori_loop(0, rows // NORM_RC, stats, 0)
    gain = g_ref[...]
    scale1 = 1.0 + sc_ref[...]
    shift = sh_ref[...]

    def apply(i, carry):
        r0 = pl.multiple_of(i * NORM_RC, NORM_RC)
        y = x_ref[pl.ds(r0, NORM_RC), :] * inv_ref[pl.ds(r0, NORM_RC), :]
        dst_ref[pl.ds(r0, NORM_RC), :] = ((y * gain) * scale1 + shift).astype(BF16)
        return carry

    lax.fori_loop(0, rows // NORM_RC, apply, 0)


IN_TN = 512
N_IN_BLK = D_IN // IN_TN
IN_Q_BLKS = D_NA // IN_TN
IN_QK_BLKS = 2 * IN_Q_BLKS
IN_QKV_BLKS = 3 * IN_Q_BLKS
KV_TILE_B = TM // SEQ


def _inproj_kernel(with_kv, x_ref, g_ref, sc_ref, sh_ref, w_ref, qkg_ref, ones_ref, cs_ref, *rest):
    if with_kv:
        h_ref, p_ref, ac_ref, as_ref, k_ref, v_ref, inv_s = rest[-7:]
    else:
        h_ref, p_ref, ac_ref, as_ref, inv_s = rest[-5:]
    n = pl.program_id(1)

    @pl.when(n == 0)
    def _():
        _norm_mod_rows(x_ref, g_ref, sc_ref, sh_ref, h_ref, inv_s, TM)

    y = _bdot(h_ref[...], w_ref[...].astype(BF16))

    @pl.when(n < IN_QK_BLKS)
    def _():
        ss = _bdot((y * y).astype(BF16), ones_ref[...])
        yn = y * lax.rsqrt(ss * (1.0 / HEAD_DIM) + EPS) * qkg_ref[...]
        p_ref[...] = yn.astype(BF16)
        if with_kv:
            @pl.when(n >= IN_Q_BLKS)
            def _():
                k_ref[...] = yn.reshape(KV_TILE_B, SEQ, IN_TN)

    @pl.when(jnp.logical_and(n >= IN_QK_BLKS, n < IN_QKV_BLKS))
    def _():
        p_ref[...] = y.astype(BF16)
        if with_kv:
            v_ref[...] = y.reshape(KV_TILE_B, SEQ, IN_TN)

    @pl.when(n >= IN_QKV_BLKS)
    def _():
        cs = cs_ref[...]
        for gg in range(IN_TN // F_GROUP):
            a = _bdot(y[:, gg * F_GROUP:(gg + 1) * F_GROUP].astype(BF16), cs)
            ac_ref[:, gg * F_GROUP:(gg + 1) * F_GROUP] = a[:, :F_GROUP].astype(BF16)
            as_ref[:, gg * F_GROUP:(gg + 1) * F_GROUP] = a[:, F_GROUP:].astype(BF16)


def _in_projection(layer, path, x, mod, norm_g, w_in, qkg, ones_bd, cs, kv_prev=None):
    with_kv = path == CTX
    u_blk = lambda n: jnp.clip(n - IN_QKV_BLKS, 0, 1)
    in_specs = [
        pl.BlockSpec((TM, D_MODEL), lambda m, n: (m, 0)),
        pl.BlockSpec((None, 1, D_MODEL), lambda m, n: (layer, 0, 0)),
        _mod_spec(layer, path, 1, D_MODEL, lambda n: 0),
        _mod_spec(layer, path, 0, D_MODEL, lambda n: 0),
        pl.BlockSpec((None, D_MODEL, IN_TN), lambda m, n: (layer, 0, n)),
        pl.BlockSpec((None, None, 1, IN_TN),
                     lambda m, n: (layer, jnp.clip(n // IN_Q_BLKS, 0, 1), 0, 0)),
        pl.BlockSpec((IN_TN, IN_TN), lambda m, n: (0, 0)),
        pl.BlockSpec((F_GROUP, 2 * F_GROUP), lambda m, n: (0, 0)),
    ]
    args = [x, norm_g.reshape(DEPTH, 1, D_MODEL), mod, mod, w_in, qkg, ones_bd, cs]
    out_specs = [
        pl.BlockSpec((TM, D_MODEL), lambda m, n: (m, 0)),
        pl.BlockSpec((TM, IN_TN), lambda m, n: (m, jnp.minimum(n, IN_QKV_BLKS - 1))),
        pl.BlockSpec((TM, IN_TN), lambda m, n: (m, u_blk(n))),
        pl.BlockSpec((TM, IN_TN), lambda m, n: (m, u_blk(n))),
    ]
    out_shape = [
        jax.ShapeDtypeStruct((N_TOK, D_MODEL), BF16),
        jax.ShapeDtypeStruct((N_TOK, 3 * D_NA), BF16),
        jax.ShapeDtypeStruct((N_TOK, D_F), BF16),
        jax.ShapeDtypeStruct((N_TOK, D_F), BF16),
    ]
    aliases = {}
    if with_kv:
        kv_spec = lambda first: pl.BlockSpec(
            (KV_TILE_B, None, SEQ, IN_TN),
            lambda m, n: (m, layer, 0, jnp.clip(n - first, 0, IN_Q_BLKS - 1)))
        out_specs += [kv_spec(IN_Q_BLKS), kv_spec(IN_QK_BLKS)]
        out_shape += [jax.ShapeDtypeStruct((BATCH, DEPTH, SEQ, D_NA), F32)] * 2
        in_specs += [pl.BlockSpec(memory_space=pl.ANY)] * 2
        aliases = {len(args): 4, len(args) + 1: 5}
        args += list(kv_prev)

    def body(*refs):
        _inproj_kernel(with_kv, *refs)

    return pl.pallas_call(
        body,
        grid=(N_MT, N_IN_BLK),
        in_specs=in_specs,
        out_specs=out_specs,
        out_shape=out_shape,
        scratch_shapes=[pltpu.VMEM((TM, 1), F32)],
        input_output_aliases=aliases,
        compiler_params=_params(2),
        name=f"in_projection_{path}_l{layer}",
    )(*args)


def _stack_heads(q):
    lane = lax.broadcasted_iota(jnp.int32, q.shape, 1)
    zero = jnp.zeros_like(q)
    return jnp.concatenate([jnp.where(lane < HEAD_DIM, q, zero),
                            jnp.where(lane >= HEAD_DIM, q, zero)], axis=0)


def _unstack_heads(o2):
    rows = o2.shape[0] // 2
    lane = lax.broadcasted_iota(jnp.int32, (rows, V7X_LANES), 1)
    return jnp.where(lane < HEAD_DIM, o2[:rows], o2[rows:])


def _qk(q2, k):
    return lax.dot_general(q2, k, (((1,), (1,)), ((), ())), preferred_element_type=F32)


def _softmax(scores):
    mx = scores[0].max(axis=-1, keepdims=True)
    for s in scores[1:]:
        mx = jnp.maximum(mx, s.max(axis=-1, keepdims=True))
    exps = [jnp.exp(s - mx) for s in scores]
    den = exps[0].sum(axis=-1, keepdims=True)
    for e in exps[1:]:
        den = den + e.sum(axis=-1, keepdims=True)
    inv = 1.0 / den
    return [(e * inv).astype(BF16) for e in exps]


def _ctx_attn_kernel(q_ref, k_ref, v_ref, o_ref):
    for p in range(N_HEADS // 2):
        cols = slice(p * V7X_LANES, (p + 1) * V7X_LANES)
        q2 = _stack_heads(q_ref[:, cols])
        prob, = _softmax([_qk(q2, k_ref[:, cols])])
        o2 = _bdot(prob, v_ref[:, cols])
        o_ref[:, cols] = _unstack_heads(o2).astype(BF16)


def _context_attention(layer, p):
    return pl.pallas_call(
        _ctx_attn_kernel,
        grid=(BATCH,),
        in_specs=[pl.BlockSpec((SEQ, D_NA), lambda b: (b, 0)),
                  pl.BlockSpec((SEQ, D_NA), lambda b: (b, 1)),
                  pl.BlockSpec((SEQ, D_NA), lambda b: (b, 2))],
        out_specs=pl.BlockSpec((SEQ, D_NA), lambda b: (b, 0)),
        out_shape=jax.ShapeDtypeStruct((N_TOK, D_NA), BF16),
        compiler_params=_params(1),
        name=f"context_attention_l{layer}",
    )(p, p, p)


NA_QROWS = 4
NA_KROWS = 12
NA_KSTART = (0, 0, 4, 4)


def _na_bias_strip(tbl_ref, group, strip):
    hh, rr = divmod(strip, NA_QROWS)
    ks = NA_KSTART[group]
    lane = lax.broadcasted_iota(jnp.int32, (GRID_W, V7X_LANES), 1)
    ninf = jnp.full((GRID_W, V7X_LANES), NEG_INF, F32)
    r = NA_QROWS * group + rr
    rs = min(max(r - KH // 2, 0), ROWS - KH)
    tiles = []
    for a in range(NA_KROWS // 2):
        kk0 = ks + 2 * a
        ok0 = rs <= kk0 < rs + KH
        ok1 = rs <= kk0 + 1 < rs + KH
        dr0 = kk0 - r + KH - 1
        if ok0 and ok1:
            t = tbl_ref[hh, dr0]
        elif ok0:
            t = jnp.where(lane < GRID_W, tbl_ref[hh, dr0], ninf)
        elif ok1:
            t = jnp.where(lane >= GRID_W, tbl_ref[hh, dr0], ninf)
        else:
            t = ninf
        tiles.append(t)
    return jnp.concatenate(tiles, axis=1)


NA_PAIRS = 4
NA_LANES = NA_PAIRS * V7X_LANES


def _lat_attn_kernel(q_ref, k_ref, v_ref, kc_ref, vc_ref, tbl_ref, o_ref):
    qrows = NA_QROWS * GRID_W
    for pr in range(NA_PAIRS):
        cols = slice(pr * V7X_LANES, (pr + 1) * V7X_LANES)
        kc = kc_ref[:, cols].astype(BF16)
        vc = vc_ref[:, cols].astype(BF16)
        tbl = tbl_ref.at[2 * pr:2 * pr + 2]
        for g in range(ROWS // NA_QROWS):
            k0 = NA_KSTART[g] * GRID_W
            q2 = _stack_heads(q_ref[g * qrows:(g + 1) * qrows, cols])
            bias = jnp.concatenate([_na_bias_strip(tbl, g, t) for t in range(2 * NA_QROWS)], axis=0)
            s_loc = _qk(q2, k_ref[k0:k0 + NA_KROWS * GRID_W, cols]) + bias
            p_loc, p_ctx = _softmax([s_loc, _qk(q2, kc)])
            o2 = _bdot(p_loc, v_ref[k0:k0 + NA_KROWS * GRID_W, cols]) + _bdot(p_ctx, vc)
            o_ref[g * qrows:(g + 1) * qrows, cols] = _unstack_heads(o2).astype(BF16)


def _latent_attention(layer, p, cache_k, cache_v, tbl):
    n_blk = D_NA // NA_LANES
    cache_spec = pl.BlockSpec((None, None, PAST_LEN, NA_LANES), lambda b, h: (b, layer, 0, h))
    return pl.pallas_call(
        _lat_attn_kernel,
        grid=(DEC_BATCH, n_blk),
        in_specs=[
            pl.BlockSpec((DEC_SEQ, NA_LANES), lambda b, h: (b, h)),
            pl.BlockSpec((DEC_SEQ, NA_LANES), lambda b, h: (b, n_blk + h)),
            pl.BlockSpec((DEC_SEQ, NA_LANES), lambda b, h: (b, 2 * n_blk + h)),
            cache_spec, cache_spec,
            pl.BlockSpec((None, 2 * NA_PAIRS, N_DR, GRID_W, V7X_LANES), lambda b, h: (layer, h, 0, 0, 0)),
        ],
        out_specs=pl.BlockSpec((DEC_SEQ, NA_LANES), lambda b, h: (b, h)),
        out_shape=jax.ShapeDtypeStruct((N_TOK, D_NA), BF16),
        compiler_params=_params(2),
        name=f"latent_attention_l{layer}",
    )(p, p, p, cache_k, cache_v, tbl)


def _dft_tables(n, scale):
    j = np.arange(n, dtype=np.int64)
    ang = 2.0 * np.pi * ((j[:, None] * j[None, :]) % n).astype(np.float64) / n
    return (np.cos(ang) * scale).astype(np.float32), (np.sin(ang) * scale).astype(np.float32)


DFT_ROWS = 1024


def _pos_dft_kernel(seq, ct_ref, st_ref, ac_ref, as_ref, o_ref):
    ct = ct_ref[...]
    st = st_ref[...]
    for i in range(DFT_ROWS // seq):
        rows = slice(i * seq, (i + 1) * seq)
        o_ref[rows, :] = (_bdot(ct, ac_ref[rows, :]) - _bdot(st, as_ref[rows, :])).astype(BF16)


def _position_dft(layer, seq, a_cos, a_sin):
    ct, st = _dft_tables(seq, seq ** -0.5)
    blk = pl.BlockSpec((DFT_ROWS, D_F), lambda b: (b, 0))
    tab = pl.BlockSpec((seq, seq), lambda b: (0, 0))

    def body(*refs):
        _pos_dft_kernel(seq, *refs)

    return pl.pallas_call(
        body,
        grid=(N_TOK // DFT_ROWS,),
        in_specs=[tab, tab, blk, blk],
        out_specs=blk,
        out_shape=jax.ShapeDtypeStruct((N_TOK, D_F), BF16),
        compiler_params=_params(1),
        name=f"position_dft_{seq}_l{layer}",
    )(jnp.asarray(ct).astype(BF16), jnp.asarray(st).astype(BF16), a_cos, a_sin)


MIX_TC = 256
N_MIX_BLK = D_MODEL // MIX_TC
OUT_TN = 256
N_OUT_BLK = D_MODEL // OUT_TN


def _mix_kernel(h_ref, a_ref, f_ref, wgn_ref, wgf_ref, wna_ref, wf_ref, wo_ref, x_ref, g1_ref,
                o_ref, mix_s, wo_s):
    m = pl.program_id(0)
    s = pl.program_id(1)

    @pl.when(s < N_MIX_BLK)
    def _():
        h = h_ref[...]
        g_na = _sigmoid(_bdot(h, wgn_ref[...].astype(BF16)))
        g_fn = _sigmoid(_bdot(h, wgf_ref[...].astype(BF16)))
        na = _bdot(a_ref[...], wna_ref[...].astype(BF16))
        fn = _bdot(f_ref[...], wf_ref[...].astype(BF16))
        mix_s[s] = (g_na * na + g_fn * fn).astype(BF16)

    @pl.when(s >= N_MIX_BLK)
    def _():
        n = s - N_MIX_BLK

        @pl.when(m == 0)
        def _():
            wo_s[n] = wo_ref[...].astype(BF16)

        acc = None
        for c in range(N_MIX_BLK):
            t = _bdot(mix_s[c], wo_s[n, c * MIX_TC:(c + 1) * MIX_TC, :])
            acc = t if acc is None else acc + t
        o_ref[...] = x_ref[...] + g1_ref[...] * acc


def _token_mixing(layer, path, h, attn, f, x, mod, w_gate, w_na, w_f, w_o):
    mix_blk = lambda s: jnp.minimum(s, N_MIX_BLK - 1)
    out_blk = lambda s: jnp.clip(s - N_MIX_BLK, 0, N_OUT_BLK - 1)
    return pl.pallas_call(
        _mix_kernel,
        grid=(N_MT, N_MIX_BLK + N_OUT_BLK),
        in_specs=[
            pl.BlockSpec((TM, D_MODEL), lambda m, s: (m, 0)),
            pl.BlockSpec((TM, D_NA), lambda m, s: (m, 0)),
            pl.BlockSpec((TM, D_F), lambda m, s: (m, 0)),
            pl.BlockSpec((None, D_MODEL, MIX_TC), lambda m, s: (layer, 0, mix_blk(s))),
            pl.BlockSpec((None, D_MODEL, MIX_TC), lambda m, s: (layer, 0, N_MIX_BLK + mix_blk(s))),
            pl.BlockSpec((None, D_NA, MIX_TC), lambda m, s: (layer, 0, mix_blk(s))),
            pl.BlockSpec((None, D_F, MIX_TC), lambda m, s: (layer, 0, mix_blk(s))),
            pl.BlockSpec((None, D_MODEL, OUT_TN),
                         lambda m, s: (layer, 0, jnp.where(m == 0, out_blk(s), N_OUT_BLK - 1))),
            pl.BlockSpec((TM, OUT_TN), lambda m, s: (m, out_blk(s))),
            _mod_spec(layer, path, 2, OUT_TN, out_blk),
        ],
        out_specs=pl.BlockSpec((TM, OUT_TN), lambda m, s: (m, out_blk(s))),
        out_shape=jax.ShapeDtypeStruct((N_TOK, D_MODEL), F32),
        scratch_shapes=[pltpu.VMEM((N_MIX_BLK, TM, MIX_TC), BF16),
                        pltpu.VMEM((N_OUT_BLK, D_MODEL, OUT_TN), BF16)],
        compiler_params=_params(2),
        name=f"token_mixing_{path}_l{layer}",
    )(h, attn, f, w_gate, w_gate, w_na, w_f, w_o, x, mod)


NORM_TM = 512


def _norm2_kernel(x_ref, g_ref, sc_ref, sh_ref, o_ref, inv_s):
    _norm_mod_rows(x_ref, g_ref, sc_ref, sh_ref, o_ref, inv_s, NORM_TM)


def _norm_mod(layer, path, x, mod, norm_g):
    spec = lambda chunk: pl.BlockSpec((None, None, None, 1, D_MODEL),
                                      lambda m: (layer, _mod_row(path, m, NORM_TM), chunk, 0, 0))
    return pl.pallas_call(
        _norm2_kernel,
        grid=(N_TOK // NORM_TM,),
        in_specs=[pl.BlockSpec((NORM_TM, D_MODEL), lambda m: (m, 0)),
                  pl.BlockSpec((None, 1, D_MODEL), lambda m: (layer, 0, 0)),
                  spec(4), spec(3)],
        out_specs=pl.BlockSpec((NORM_TM, D_MODEL), lambda m: (m, 0)),
        out_shape=jax.ShapeDtypeStruct((N_TOK, D_MODEL), BF16),
        scratch_shapes=[pltpu.VMEM((NORM_TM, 1), F32)],
        compiler_params=_params(1),
        name=f"norm_mod2_{path}_l{layer}",
    )(x, norm_g.reshape(DEPTH, 1, D_MODEL), mod, mod)


FF_TC = 256
N_FF_BLK = D_FF // FF_TC
DOWN_TN = 256
N_DOWN_BLK = D_MODEL // DOWN_TN


def _ffn_kernel(h_ref, wa_ref, wg_ref, wd_ref, x_ref, g2_ref, o_ref, act_s):
    s = pl.program_id(1)

    @pl.when(s < N_FF_BLK)
    def _():
        h = h_ref[...]
        a = _bdot(h, wa_ref[...].astype(BF16))
        g = _bdot(h, wg_ref[...].astype(BF16))
        act_s[s] = ((g * _sigmoid(g)) * a).astype(BF16)

    @pl.when(s >= N_FF_BLK)
    def _():
        acc = None
        for j in range(N_FF_BLK):
            t = _bdot(act_s[j], wd_ref[j * FF_TC:(j + 1) * FF_TC, :].astype(BF16))
            acc = t if acc is None else acc + t
        o_ref[...] = x_ref[...] + g2_ref[...] * acc


def _ffn(layer, path, h2, x, mod, w_gate_up, w_down):
    ff_blk = lambda s: jnp.minimum(s, N_FF_BLK - 1)
    out_blk = lambda s: jnp.clip(s - N_FF_BLK, 0, N_DOWN_BLK - 1)
    return pl.pallas_call(
        _ffn_kernel,
        grid=(N_MT, N_FF_BLK + N_DOWN_BLK),
        in_specs=[
            pl.BlockSpec((TM, D_MODEL), lambda m, s: (m, 0)),
            pl.BlockSpec((None, D_MODEL, FF_TC), lambda m, s: (layer, 0, ff_blk(s))),
            pl.BlockSpec((None, D_MODEL, FF_TC), lambda m, s: (layer, 0, N_FF_BLK + ff_blk(s))),
            pl.BlockSpec((None, D_FF, DOWN_TN), lambda m, s: (layer, 0, out_blk(s))),
            pl.BlockSpec((TM, DOWN_TN), lambda m, s: (m, out_blk(s))),
            _mod_spec(layer, path, 5, DOWN_TN, out_blk),
        ],
        out_specs=pl.BlockSpec((TM, DOWN_TN), lambda m, s: (m, out_blk(s))),
        out_shape=jax.ShapeDtypeStruct((N_TOK, D_MODEL), F32),
        scratch_shapes=[pltpu.VMEM((N_FF_BLK, TM, FF_TC), BF16)],
        compiler_params=_params(2),
        name=f"ffn_{path}_l{layer}",
    )(h2, w_gate_up, w_gate_up, w_down, x, mod)


def kernel(x_prompt, x_sample, cache_k, cache_v, c, c_ctx, w_mod, b_mod, norm1_g, norm2_g,
           w_in, q_norm_g, k_norm_g, rpb, w_na_proj, w_fnet_proj, w_gate, w_o, w_gate_up, w_down):
    xs = {CTX: x_prompt.reshape(N_TOK, D_MODEL), LAT: x_sample.reshape(N_TOK, D_MODEL)}
    cond8 = jnp.concatenate([c_ctx[None, :], c, jnp.zeros((8 - 1 - DEC_BATCH, D_MODEL), F32)], axis=0)
    mod = _modulation(cond8, w_mod, b_mod).reshape(DEPTH, 8, 6, 1, D_MODEL)
    tbl = _bias_tables(rpb)

    reps = IN_TN // HEAD_DIM
    qkg = jnp.stack([jnp.tile(q_norm_g * (HEAD_DIM ** -0.5), (1, reps)),
                     jnp.tile(k_norm_g, (1, reps))], axis=1).reshape(DEPTH, 2, 1, IN_TN)
    head_id = np.arange(IN_TN) // HEAD_DIM
    ones_bd = jnp.asarray((head_id[:, None] == head_id[None, :]).astype(np.float32)).astype(BF16)
    cc, sc = _dft_tables(F_GROUP, F_GROUP ** -0.5)
    cs = jnp.asarray(np.concatenate([cc, sc], axis=1)).astype(BF16)

    ck = cache_k.reshape(DEC_BATCH, DEPTH, PAST_LEN, D_NA)
    cv = cache_v.reshape(DEC_BATCH, DEPTH, PAST_LEN, D_NA)

    kv = tuple(jnp.zeros((BATCH, DEPTH, SEQ, D_NA), F32) for _ in range(2))
    for l in range(DEPTH):
        for path in (CTX, LAT):
            x = xs[path]
            outs = _in_projection(l, path, x, mod, norm1_g, w_in, qkg, ones_bd, cs, kv_prev=kv)
            h, p, a_cos, a_sin = outs[:4]
            if path == CTX:
                kv = outs[4:]
                attn = _context_attention(l, p)
                f = _position_dft(l, SEQ, a_cos, a_sin)
            else:
                attn = _latent_attention(l, p, ck, cv, tbl)
                f = _position_dft(l, DEC_SEQ, a_cos, a_sin)
            x1 = _token_mixing(l, path, h, attn, f, x, mod, w_gate, w_na_proj, w_fnet_proj, w_o)
            h2 = _norm_mod(l, path, x1, mod, norm2_g)
            xs[path] = _ffn(l, path, h2, x1, mod, w_gate_up, w_down)

    new_k, new_v = (t.reshape(BATCH, DEPTH, SEQ, N_HEADS, HEAD_DIM) for t in kv)
    return (xs[CTX].reshape(BATCH, SEQ, D_MODEL), xs[LAT].reshape(DEC_BATCH, DEC_SEQ, D_MODEL),
            new_k, new_v)
```

```python
import numpy as np
import jax
import jax.numpy as jnp
from jax import lax
from jax.experimental import pallas as pl
from jax.experimental.pallas import tpu as pltpu

F32 = jnp.float32
BF16 = jnp.bfloat16

D_MODEL = 2048
BATCH = 16
SEQ = 256
DEPTH = 2
DEC_BATCH = 4
DEC_SEQ = 1024
PAST_LEN = 256
GRID_W = 64
ROWS = DEC_SEQ // GRID_W
N_HEADS = 16
HEAD_DIM = 64
D_NA = N_HEADS * HEAD_DIM
D_F = D_MODEL // 2
N_FGROUPS = 4
F_GROUP = D_F // N_FGROUPS
KH = 8
KW = 16
D_FF = 5632
D_IN = 3 * D_NA + D_F
EPS = 1e-6
N_DR = 2 * KH - 1
N_DC = 2 * KW - 1

N_TOK = BATCH * SEQ
assert N_TOK == DEC_BATCH * DEC_SEQ

V7X_LANES = 128
V7X_VMEM_LIMIT = 56 * 1024 * 1024

TM = 1024
N_MT = N_TOK // TM
NEG_INF = float("-inf")
CTX, LAT = "ctx", "lat"


def _params(n_axes):
    return pltpu.CompilerParams(dimension_semantics=("arbitrary",) * n_axes,
                                vmem_limit_bytes=V7X_VMEM_LIMIT)


def _mod_row(path, m, tile):
    return 0 if path == CTX else 1 + m // (DEC_SEQ // tile)


def _mod_spec(layer, path, chunk, width, col_map):
    return pl.BlockSpec((None, None, None, 1, width),
                        lambda m, s: (layer, _mod_row(path, m, TM), chunk, 0, col_map(s)))


def _sigmoid(z):
    return 1.0 / (1.0 + jnp.exp(-z))


def _bdot(a, b):
    return jnp.dot(a, b, preferred_element_type=F32)


MOD_TN = 1024


def _mod_kernel(cond_ref, w_ref, b_ref, o_ref):
    cnd = cond_ref[...]
    s = (cnd * _sigmoid(cnd)).astype(BF16)
    o_ref[...] = _bdot(s, w_ref[...].astype(BF16)) + b_ref[...]


def _modulation(cond8, w_mod, b_mod):
    n_blk = 6 * D_MODEL // MOD_TN
    return pl.pallas_call(
        _mod_kernel,
        grid=(DEPTH, n_blk),
        in_specs=[
            pl.BlockSpec((8, D_MODEL), lambda l, n: (0, 0)),
            pl.BlockSpec((None, D_MODEL, MOD_TN), lambda l, n: (l, 0, n)),
            pl.BlockSpec((None, 1, MOD_TN), lambda l, n: (l, 0, n)),
        ],
        out_specs=pl.BlockSpec((None, 8, MOD_TN), lambda l, n: (l, 0, n)),
        out_shape=jax.ShapeDtypeStruct((DEPTH, 8, 6 * D_MODEL), F32),
        compiler_params=_params(2),
        name="modulation",
    )(cond8, w_mod, b_mod.reshape(DEPTH, 1, 6 * D_MODEL))


def _bias_onehots():
    width = GRID_W * V7X_LANES
    oh_l = np.zeros((N_DC + 1, width), np.float32)
    oh_r = np.zeros((N_DC + 1, width), np.float32)
    mask = np.full((1, width), -np.inf, np.float32)
    for c in range(GRID_W):
        ws = min(max(c - KW // 2, 0), GRID_W - KW)
        for kc in range(ws, ws + KW):
            e = kc - c + KW - 1
            oh_l[e, c * V7X_LANES + kc] = 1.0
            oh_r[e, c * V7X_LANES + GRID_W + kc] = 1.0
            mask[0, c * V7X_LANES + kc] = 0.0
            mask[0, c * V7X_LANES + GRID_W + kc] = 0.0
    mask_last = mask.copy().reshape(GRID_W, V7X_LANES)
    mask_last[:, GRID_W:] = -np.inf
    return oh_l, oh_r, mask, mask_last.reshape(1, width)


def _split3(x):
    hi = x.astype(BF16)
    r1 = x - hi.astype(F32)
    mid = r1.astype(BF16)
    lo = (r1 - mid.astype(F32)).astype(BF16)
    return hi, mid, lo


def _bias_kernel(r1_ref, r2_ref, sel_ref, ohl_ref, ohr_ref, mask_ref, maskl_ref, o_ref):
    acc = None
    for r_ref, oh_ref in ((r1_ref, ohl_ref), (r2_ref, ohr_ref)):
        oh = oh_ref[...]
        for piece in _split3(r_ref[...]):
            t = _bdot(piece, oh)
            acc = t if acc is None else acc + t
    mask = jnp.where(sel_ref[...] > 0.5, maskl_ref[...], mask_ref[...])
    o_ref[...] = acc + mask


def _bias_tables(rpb):
    oh_l, oh_r, mask, mask_last = _bias_onehots()
    rows = N_HEADS * N_DR
    width = GRID_W * V7X_LANES
    pad = jnp.zeros((DEPTH, N_HEADS, N_DR, 1), F32)
    r1 = jnp.concatenate([rpb, pad], axis=-1).reshape(DEPTH * rows, N_DC + 1)
    nxt = jnp.concatenate([rpb[:, :, 1:], jnp.zeros((DEPTH, N_HEADS, 1, N_DC), F32)], axis=2)
    r2 = jnp.concatenate([nxt, pad], axis=-1).reshape(DEPTH * rows, N_DC + 1)
    sel = np.zeros((DEPTH, N_HEADS, N_DR, 1), np.float32)
    sel[:, :, N_DR - 1] = 1.0
    sel = jnp.asarray(sel.reshape(DEPTH * rows, 1))
    const = lambda shape: pl.BlockSpec(shape, lambda l: (0, 0))
    out = pl.pallas_call(
        _bias_kernel,
        grid=(DEPTH,),
        in_specs=[
            pl.BlockSpec((rows, N_DC + 1), lambda l: (l, 0)),
            pl.BlockSpec((rows, N_DC + 1), lambda l: (l, 0)),
            pl.BlockSpec((rows, 1), lambda l: (l, 0)),
            const((N_DC + 1, width)), const((N_DC + 1, width)),
            const((1, width)), const((1, width)),
        ],
        out_specs=pl.BlockSpec((rows, width), lambda l: (l, 0)),
        out_shape=jax.ShapeDtypeStruct((DEPTH * rows, width), F32),
        compiler_params=_params(1),
        name="bias_tables",
    )(r1, r2, sel, jnp.asarray(oh_l).astype(BF16), jnp.asarray(oh_r).astype(BF16),
      jnp.asarray(mask), jnp.asarray(mask_last))
    return out.reshape(DEPTH, N_HEADS, N_DR, GRID_W, V7X_LANES)


NORM_RC = 128


def _norm_mod_rows(x_ref, g_ref, sc_ref, sh_ref, dst_ref, inv_ref, rows):
    def stats(i, carry):
        r0 = pl.multiple_of(i * NORM_RC, NORM_RC)
        x = x_ref[pl.ds(r0, NORM_RC), :]
        inv_ref[pl.ds(r0, NORM_RC), :] = lax.rsqrt(jnp.mean(x * x, axis=-1, keepdims=True) + EPS)
        return carry

    lax.fori_loop(0, rows // NORM_RC, stats, 0)
    gain = g_ref[...]
    scale1 = 1.0 + sc_ref[...]
    shift = sh_ref[...]

    def apply(i, carry):
        r0 = pl.multiple_of(i * NORM_RC, NORM_RC)
        y = x_ref[pl.ds(r0, NORM_RC), :] * inv_ref[pl.ds(r0, NORM_RC), :]
        dst_ref[pl.ds(r0, NORM_RC), :] = ((y * gain) * scale1 + shift).astype(BF16)
        return carry

    lax.fori_loop(0, rows // NORM_RC, apply, 0)


IN_TN = 512
N_IN_BLK = D_IN // IN_TN
IN_Q_BLKS = D_NA // IN_TN
IN_QK_BLKS = 2 * IN_Q_BLKS
IN_QKV_BLKS = 3 * IN_Q_BLKS
KV_TILE_B = TM // SEQ
QK_SS_W = 256


def _inproj_kernel(with_kv, x_ref, g_ref, sc_ref, sh_ref, w_ref, qkg_ref, ones_ref, cs_ref, *rest):
    if with_kv:
        h_ref, p_ref, ac_ref, as_ref, k_ref, v_ref, inv_s = rest[-7:]
    else:
        h_ref, p_ref, ac_ref, as_ref, inv_s = rest[-5:]
    n = pl.program_id(1)

    @pl.when(n == 0)
    def _():
        _norm_mod_rows(x_ref, g_ref, sc_ref, sh_ref, h_ref, inv_s, TM)

    y = _bdot(h_ref[...], w_ref[...].astype(BF16))

    @pl.when(n < IN_QK_BLKS)
    def _():
        y2 = (y * y).astype(BF16)
        ones = ones_ref[...]
        ss = jnp.concatenate([_bdot(y2[:, i * QK_SS_W:(i + 1) * QK_SS_W], ones)
                              for i in range(IN_TN // QK_SS_W)], axis=1)
        yn = y * lax.rsqrt(ss * (1.0 / HEAD_DIM) + EPS) * qkg_ref[...]
        p_ref[...] = yn.astype(BF16)
        if with_kv:
            @pl.when(n >= IN_Q_BLKS)
            def _():
                k_ref[...] = yn.reshape(KV_TILE_B, SEQ, IN_TN)

    @pl.when(jnp.logical_and(n >= IN_QK_BLKS, n < IN_QKV_BLKS))
    def _():
        p_ref[...] = y.astype(BF16)
        if with_kv:
            v_ref[...] = y.reshape(KV_TILE_B, SEQ, IN_TN)

    @pl.when(n >= IN_QKV_BLKS)
    def _():
        cs = cs_ref[...]
        for gg in range(IN_TN // F_GROUP):
            a = _bdot(y[:, gg * F_GROUP:(gg + 1) * F_GROUP].astype(BF16), cs)
            ac_ref[:, gg * F_GROUP:(gg + 1) * F_GROUP] = a[:, :F_GROUP].astype(BF16)
            as_ref[:, gg * F_GROUP:(gg + 1) * F_GROUP] = a[:, F_GROUP:].astype(BF16)


def _in_projection(layer, path, x, mod, norm_g, w_in, qkg, ones_bd, cs, kv_prev=None):
    with_kv = path == CTX
    u_blk = lambda n: jnp.clip(n - IN_QKV_BLKS, 0, 1)
    in_specs = [
        pl.BlockSpec((TM, D_MODEL), lambda m, n: (m, 0)),
        pl.BlockSpec((None, 1, D_MODEL), lambda m, n: (layer, 0, 0)),
        _mod_spec(layer, path, 1, D_MODEL, lambda n: 0),
        _mod_spec(layer, path, 0, D_MODEL, lambda n: 0),
        pl.BlockSpec((None, D_MODEL, IN_TN), lambda m, n: (layer, 0, n)),
        pl.BlockSpec((None, None, 1, IN_TN),
                     lambda m, n: (layer, jnp.clip(n // IN_Q_BLKS, 0, 1), 0, 0)),
        pl.BlockSpec((QK_SS_W, QK_SS_W), lambda m, n: (0, 0)),
        pl.BlockSpec((F_GROUP, 2 * F_GROUP), lambda m, n: (0, 0)),
    ]
    args = [x, norm_g.reshape(DEPTH, 1, D_MODEL), mod, mod, w_in, qkg, ones_bd, cs]
    out_specs = [
        pl.BlockSpec((TM, D_MODEL), lambda m, n: (m, 0)),
        pl.BlockSpec((TM, IN_TN), lambda m, n: (m, jnp.minimum(n, IN_QKV_BLKS - 1))),
        pl.BlockSpec((TM, IN_TN), lambda m, n: (m, u_blk(n))),
        pl.BlockSpec((TM, IN_TN), lambda m, n: (m, u_blk(n))),
    ]
    out_shape = [
        jax.ShapeDtypeStruct((N_TOK, D_MODEL), BF16),
        jax.ShapeDtypeStruct((N_TOK, 3 * D_NA), BF16),
        jax.ShapeDtypeStruct((N_TOK, D_F), BF16),
        jax.ShapeDtypeStruct((N_TOK, D_F), BF16),
    ]
    aliases = {}
    if with_kv:
        kv_spec = lambda first: pl.BlockSpec(
            (KV_TILE_B, None, SEQ, IN_TN),
            lambda m, n: (m, layer, 0, jnp.clip(n - first, 0, IN_Q_BLKS - 1)))
        out_specs += [kv_spec(IN_Q_BLKS), kv_spec(IN_QK_BLKS)]
        out_shape += [jax.ShapeDtypeStruct((BATCH, DEPTH, SEQ, D_NA), F32)] * 2
        in_specs += [pl.BlockSpec(memory_space=pl.ANY)] * 2
        aliases = {len(args): 4, len(args) + 1: 5}
        args += list(kv_prev)

    def body(*refs):
        _inproj_kernel(with_kv, *refs)

    return pl.pallas_call(
        body,
        grid=(N_MT, N_IN_BLK),
        in_specs=in_specs,
        out_specs=out_specs,
        out_shape=out_shape,
        scratch_shapes=[pltpu.VMEM((TM, 1), F32)],
        input_output_aliases=aliases,
        compiler_params=_params(2),
        name=f"in_projection_{path}_l{layer}",
    )(*args)


def _stack_heads(q):
    lane = lax.broadcasted_iota(jnp.int32, q.shape, 1)
    zero = jnp.zeros_like(q)
    return jnp.concatenate([jnp.where(lane < HEAD_DIM, q, zero),
                            jnp.where(lane >= HEAD_DIM, q, zero)], axis=0)


def _unstack_heads(o2):
    rows = o2.shape[0] // 2
    lane = lax.broadcasted_iota(jnp.int32, (rows, V7X_LANES), 1)
    return jnp.where(lane < HEAD_DIM, o2[:rows], o2[rows:])


def _qk(q2, k):
    return lax.dot_general(q2, k, (((1,), (1,)), ((), ())), preferred_element_type=F32)


def _softmax(scores):
    mx = scores[0].max(axis=-1, keepdims=True)
    for s in scores[1:]:
        mx = jnp.maximum(mx, s.max(axis=-1, keepdims=True))
    exps = [jnp.exp(s - mx) for s in scores]
    den = exps[0].sum(axis=-1, keepdims=True)
    for e in exps[1:]:
        den = den + e.sum(axis=-1, keepdims=True)
    inv = 1.0 / den
    return [(e * inv).astype(BF16) for e in exps]


CTX_STEP_B = 2


def _ctx_attn_kernel(q_ref, k_ref, v_ref, o_ref):
    for b in range(CTX_STEP_B):
        rows = slice(b * SEQ, (b + 1) * SEQ)
        for p in range(N_HEADS // 2):
            cols = slice(p * V7X_LANES, (p + 1) * V7X_LANES)
            q2 = _stack_heads(q_ref[rows, cols])
            prob, = _softmax([_qk(q2, k_ref[rows, cols])])
            o2 = _bdot(prob, v_ref[rows, cols])
            o_ref[rows, cols] = _unstack_heads(o2).astype(BF16)


def _context_attention(layer, p):
    blk = CTX_STEP_B * SEQ
    return pl.pallas_call(
        _ctx_attn_kernel,
        grid=(BATCH // CTX_STEP_B,),
        in_specs=[pl.BlockSpec((blk, D_NA), lambda b: (b, 0)),
                  pl.BlockSpec((blk, D_NA), lambda b: (b, 1)),
                  pl.BlockSpec((blk, D_NA), lambda b: (b, 2))],
        out_specs=pl.BlockSpec((blk, D_NA), lambda b: (b, 0)),
        out_shape=jax.ShapeDtypeStruct((N_TOK, D_NA), BF16),
        compiler_params=_params(1),
        name=f"context_attention_l{layer}",
    )(p, p, p)


NA_QROWS = 4
NA_KROWS = 12
NA_KSTART = (0, 0, 4, 4)


def _na_bias_strip(tbl_ref, group, strip):
    hh, rr = divmod(strip, NA_QROWS)
    ks = NA_KSTART[group]
    lane = lax.broadcasted_iota(jnp.int32, (GRID_W, V7X_LANES), 1)
    ninf = jnp.full((GRID_W, V7X_LANES), NEG_INF, F32)
    r = NA_QROWS * group + rr
    rs = min(max(r - KH // 2, 0), ROWS - KH)
    tiles = []
    for a in range(NA_KROWS // 2):
        kk0 = ks + 2 * a
        ok0 = rs <= kk0 < rs + KH
        ok1 = rs <= kk0 + 1 < rs + KH
        dr0 = kk0 - r + KH - 1
        if ok0 and ok1:
            t = tbl_ref[hh, dr0]
        elif ok0:
            t = jnp.where(lane < GRID_W, tbl_ref[hh, dr0], ninf)
        elif ok1:
            t = jnp.where(lane >= GRID_W, tbl_ref[hh, dr0], ninf)
        else:
            t = ninf
        tiles.append(t)
    return jnp.concatenate(tiles, axis=1)


NA_PAIRS = 4
NA_LANES = NA_PAIRS * V7X_LANES


def _lat_attn_kernel(q_ref, k_ref, v_ref, kc_ref, vc_ref, tbl_ref, o_ref):
    qrows = NA_QROWS * GRID_W
    for pr in range(NA_PAIRS):
        cols = slice(pr * V7X_LANES, (pr + 1) * V7X_LANES)
        kc = kc_ref[:, cols].astype(BF16)
        vc = vc_ref[:, cols].astype(BF16)
        tbl = tbl_ref.at[2 * pr:2 * pr + 2]
        for g in range(ROWS // NA_QROWS):
            k0 = NA_KSTART[g] * GRID_W
            q2 = _stack_heads(q_ref[g * qrows:(g + 1) * qrows, cols])
            bias = jnp.concatenate([_na_bias_strip(tbl, g, t) for t in range(2 * NA_QROWS)], axis=0)
            s_loc = _qk(q2, k_ref[k0:k0 + NA_KROWS * GRID_W, cols]) + bias
            p_loc, p_ctx = _softmax([s_loc, _qk(q2, kc)])
            o2 = _bdot(p_loc, v_ref[k0:k0 + NA_KROWS * GRID_W, cols]) + _bdot(p_ctx, vc)
            o_ref[g * qrows:(g + 1) * qrows, cols] = _unstack_heads(o2).astype(BF16)


def _latent_attention(layer, p, cache_k, cache_v, tbl):
    n_blk = D_NA // NA_LANES
    cache_spec = pl.BlockSpec((None, None, PAST_LEN, NA_LANES), lambda b, h: (b, layer, 0, h))
    return pl.pallas_call(
        _lat_attn_kernel,
        grid=(DEC_BATCH, n_blk),
        in_specs=[
            pl.BlockSpec((DEC_SEQ, NA_LANES), lambda b, h: (b, h)),
            pl.BlockSpec((DEC_SEQ, NA_LANES), lambda b, h: (b, n_blk + h)),
            pl.BlockSpec((DEC_SEQ, NA_LANES), lambda b, h: (b, 2 * n_blk + h)),
            cache_spec, cache_spec,
            pl.BlockSpec((None, 2 * NA_PAIRS, N_DR, GRID_W, V7X_LANES), lambda b, h: (layer, h, 0, 0, 0)),
        ],
        out_specs=pl.BlockSpec((DEC_SEQ, NA_LANES), lambda b, h: (b, h)),
        out_shape=jax.ShapeDtypeStruct((N_TOK, D_NA), BF16),
        compiler_params=_params(2),
        name=f"latent_attention_l{layer}",
    )(p, p, p, cache_k, cache_v, tbl)


def _dft_tables(n, scale):
    j = np.arange(n, dtype=np.int64)
    ang = 2.0 * np.pi * ((j[:, None] * j[None, :]) % n).astype(np.float64) / n
    return (np.cos(ang) * scale).astype(np.float32), (np.sin(ang) * scale).astype(np.float32)


DFT_ROWS = 1024


def _pos_dft_kernel(seq, ct_ref, st_ref, ac_ref, as_ref, o_ref):
    ct = ct_ref[...]
    st = st_ref[...]
    for i in range(DFT_ROWS // seq):
        rows = slice(i * seq, (i + 1) * seq)
        o_ref[rows, :] = (_bdot(ct, ac_ref[rows, :]) - _bdot(st, as_ref[rows, :])).astype(BF16)


def _position_dft(layer, seq, a_cos, a_sin):
    ct, st = _dft_tables(seq, seq ** -0.5)
    blk = pl.BlockSpec((DFT_ROWS, D_F), lambda b: (b, 0))
    tab = pl.BlockSpec((seq, seq), lambda b: (0, 0))

    def body(*refs):
        _pos_dft_kernel(seq, *refs)

    return pl.pallas_call(
        body,
        grid=(N_TOK // DFT_ROWS,),
        in_specs=[tab, tab, blk, blk],
        out_specs=blk,
        out_shape=jax.ShapeDtypeStruct((N_TOK, D_F), BF16),
        compiler_params=_params(1),
        name=f"position_dft_{seq}_l{layer}",
    )(jnp.asarray(ct).astype(BF16), jnp.asarray(st).astype(BF16), a_cos, a_sin)


MIX_TC = 256
N_MIX_BLK = D_MODEL // MIX_TC
OUT_TN = 256
N_OUT_BLK = D_MODEL // OUT_TN


def _mix_kernel(h_ref, a_ref, f_ref, wgn_ref, wgf_ref, wna_ref, wf_ref, wo_ref, x_ref, g1_ref,
                o_ref, mix_s, wo_s):
    m = pl.program_id(0)
    s = pl.program_id(1)

    @pl.when(s < N_MIX_BLK)
    def _():
        h = h_ref[...]
        g_na = _sigmoid(_bdot(h, wgn_ref[...].astype(BF16)))
        g_fn = _sigmoid(_bdot(h, wgf_ref[...].astype(BF16)))
        na = _bdot(a_ref[...], wna_ref[...].astype(BF16))
        fn = _bdot(f_ref[...], wf_ref[...].astype(BF16))
        mix_s[s] = (g_na * na + g_fn * fn).astype(BF16)

    @pl.when(s >= N_MIX_BLK)
    def _():
        n = s - N_MIX_BLK

        @pl.when(m == 0)
        def _():
            wo_s[n] = wo_ref[...].astype(BF16)

        acc = None
        for c in range(N_MIX_BLK):
            t = _bdot(mix_s[c], wo_s[n, c * MIX_TC:(c + 1) * MIX_TC, :])
            acc = t if acc is None else acc + t
        o_ref[...] = x_ref[...] + g1_ref[...] * acc


def _token_mixing(layer, path, h, attn, f, x, mod, w_gate, w_na, w_f, w_o):
    mix_blk = lambda s: jnp.minimum(s, N_MIX_BLK - 1)
    out_blk = lambda s: jnp.clip(s - N_MIX_BLK, 0, N_OUT_BLK - 1)
    return pl.pallas_call(
        _mix_kernel,
        grid=(N_MT, N_MIX_BLK + N_OUT_BLK),
        in_specs=[
            pl.BlockSpec((TM, D_MODEL), lambda m, s: (m, 0)),
            pl.BlockSpec((TM, D_NA), lambda m, s: (m, 0)),
            pl.BlockSpec((TM, D_F), lambda m, s: (m, 0)),
            pl.BlockSpec((None, D_MODEL, MIX_TC), lambda m, s: (layer, 0, mix_blk(s))),
            pl.BlockSpec((None, D_MODEL, MIX_TC), lambda m, s: (layer, 0, N_MIX_BLK + mix_blk(s))),
            pl.BlockSpec((None, D_NA, MIX_TC), lambda m, s: (layer, 0, mix_blk(s))),
            pl.BlockSpec((None, D_F, MIX_TC), lambda m, s: (layer, 0, mix_blk(s))),
            pl.BlockSpec((None, D_MODEL, OUT_TN),
                         lambda m, s: (layer, 0, jnp.where(m == 0, out_blk(s), N_OUT_BLK - 1))),
            pl.BlockSpec((TM, OUT_TN), lambda m, s: (m, out_blk(s))),
            _mod_spec(layer, path, 2, OUT_TN, out_blk),
        ],
        out_specs=pl.BlockSpec((TM, OUT_TN), lambda m, s: (m, out_blk(s))),
        out_shape=jax.ShapeDtypeStruct((N_TOK, D_MODEL), F32),
        scratch_shapes=[pltpu.VMEM((N_MIX_BLK, TM, MIX_TC), BF16),
                        pltpu.VMEM((N_OUT_BLK, D_MODEL, OUT_TN), BF16)],
        compiler_params=_params(2),
        name=f"token_mixing_{path}_l{layer}",
    )(h, attn, f, w_gate, w_gate, w_na, w_f, w_o, x, mod)


FF_TC = 256
N_FF_BLK = D_FF // FF_TC
DOWN_TN = 256
N_DOWN_BLK = D_MODEL // DOWN_TN


def _ffn_kernel(xf_ref, ng_ref, sc_ref, sh_ref, wa_ref, wg_ref, wd_ref, x_ref, g2_ref, o_ref,
                act_s, h_s, inv_s):
    s = pl.program_id(1)

    @pl.when(s == 0)
    def _():
        _norm_mod_rows(xf_ref, ng_ref, sc_ref, sh_ref, h_s, inv_s, TM)

    @pl.when(s < N_FF_BLK)
    def _():
        h = h_s[...]
        a = _bdot(h, wa_ref[...].astype(BF16))
        g = _bdot(h, wg_ref[...].astype(BF16))
        act_s[s] = ((g * _sigmoid(g)) * a).astype(BF16)

    @pl.when(s >= N_FF_BLK)
    def _():
        acc = None
        for j in range(N_FF_BLK):
            t = _bdot(act_s[j], wd_ref[j * FF_TC:(j + 1) * FF_TC, :].astype(BF16))
            acc = t if acc is None else acc + t
        o_ref[...] = x_ref[...] + g2_ref[...] * acc


def _ffn(layer, path, x, mod, norm_g, w_gate_up, w_down):
    ff_blk = lambda s: jnp.minimum(s, N_FF_BLK - 1)
    out_blk = lambda s: jnp.clip(s - N_FF_BLK, 0, N_DOWN_BLK - 1)
    return pl.pallas_call(
        _ffn_kernel,
        grid=(N_MT, N_FF_BLK + N_DOWN_BLK),
        in_specs=[
            pl.BlockSpec((TM, D_MODEL), lambda m, s: (m, 0)),
            pl.BlockSpec((None, 1, D_MODEL), lambda m, s: (layer, 0, 0)),
            _mod_spec(layer, path, 4, D_MODEL, lambda s: 0),
            _mod_spec(layer, path, 3, D_MODEL, lambda s: 0),
            pl.BlockSpec((None, D_MODEL, FF_TC), lambda m, s: (layer, 0, ff_blk(s))),
            pl.BlockSpec((None, D_MODEL, FF_TC), lambda m, s: (layer, 0, N_FF_BLK + ff_blk(s))),
            pl.BlockSpec((None, D_FF, DOWN_TN), lambda m, s: (layer, 0, out_blk(s))),
            pl.BlockSpec((TM, DOWN_TN), lambda m, s: (m, out_blk(s))),
            _mod_spec(layer, path, 5, DOWN_TN, out_blk),
        ],
        out_specs=pl.BlockSpec((TM, DOWN_TN), lambda m, s: (m, out_blk(s))),
        out_shape=jax.ShapeDtypeStruct((N_TOK, D_MODEL), F32),
        scratch_shapes=[pltpu.VMEM((N_FF_BLK, TM, FF_TC), BF16),
                        pltpu.VMEM((TM, D_MODEL), BF16),
                        pltpu.VMEM((TM, 1), F32)],
        compiler_params=_params(2),
        name=f"ffn_{path}_l{layer}",
    )(x, norm_g.reshape(DEPTH, 1, D_MODEL), mod, mod, w_gate_up, w_gate_up, w_down, x, mod)


def kernel(x_prompt, x_sample, cache_k, cache_v, c, c_ctx, w_mod, b_mod, norm1_g, norm2_g,
           w_in, q_norm_g, k_norm_g, rpb, w_na_proj, w_fnet_proj, w_gate, w_o, w_gate_up, w_down):
    xs = {CTX: x_prompt.reshape(N_TOK, D_MODEL), LAT: x_sample.reshape(N_TOK, D_MODEL)}
    cond8 = jnp.concatenate([c_ctx[None, :], c, jnp.zeros((8 - 1 - DEC_BATCH, D_MODEL), F32)], axis=0)
    mod = _modulation(cond8, w_mod, b_mod).reshape(DEPTH, 8, 6, 1, D_MODEL)
    tbl = _bias_tables(rpb)

    reps = IN_TN // HEAD_DIM
    qkg = jnp.stack([jnp.tile(q_norm_g * (HEAD_DIM ** -0.5), (1, reps)),
                     jnp.tile(k_norm_g, (1, reps))], axis=1).reshape(DEPTH, 2, 1, IN_TN)
    head_id = np.arange(QK_SS_W) // HEAD_DIM
    ones_bd = jnp.asarray((head_id[:, None] == head_id[None, :]).astype(np.float32)).astype(BF16)
    cc, sc = _dft_tables(F_GROUP, F_GROUP ** -0.5)
    cs = jnp.asarray(np.concatenate([cc, sc], axis=1)).astype(BF16)

    ck = cache_k.reshape(DEC_BATCH, DEPTH, PAST_LEN, D_NA)
    cv = cache_v.reshape(DEC_BATCH, DEPTH, PAST_LEN, D_NA)

    kv = tuple(jnp.zeros((BATCH, DEPTH, SEQ, D_NA), F32) for _ in range(2))
    for l in range(DEPTH):
        for path in (CTX, LAT):
            x = xs[path]
            outs = _in_projection(l, path, x, mod, norm1_g, w_in, qkg, ones_bd, cs, kv_prev=kv)
            h, p, a_cos, a_sin = outs[:4]
            if path == CTX:
                kv = outs[4:]
                attn = _context_attention(l, p)
                f = _position_dft(l, SEQ, a_cos, a_sin)
            else:
                attn = _latent_attention(l, p, ck, cv, tbl)
                f = _position_dft(l, DEC_SEQ, a_cos, a_sin)
            x1 = _token_mixing(l, path, h, attn, f, x, mod, w_gate, w_na_proj, w_fnet_proj, w_o)
            xs[path] = _ffn(l, path, x1, mod, norm2_g, w_gate_up, w_down)

    new_k, new_v = (t.reshape(BATCH, DEPTH, SEQ, N_HEADS, HEAD_DIM) for t in kv)
    return (xs[CTX].reshape(BATCH, SEQ, D_MODEL), xs[LAT].reshape(DEC_BATCH, DEC_SEQ, D_MODEL),
            new_k, new_v)
```

```python
import numpy as np
import jax
import jax.numpy as jnp
from jax import lax
from jax.experimental import pallas as pl
from jax.experimental.pallas import tpu as pltpu

F32 = jnp.float32
BF16 = jnp.bfloat16

D_MODEL = 2048
BATCH = 16
SEQ = 256
DEPTH = 2
DEC_BATCH = 4
DEC_SEQ = 1024
PAST_LEN = 256
GRID_W = 64
ROWS = DEC_SEQ // GRID_W
N_HEADS = 16
HEAD_DIM = 64
D_NA = N_HEADS * HEAD_DIM
D_F = D_MODEL // 2
N_FGROUPS = 4
F_GROUP = D_F // N_FGROUPS
KH = 8
KW = 16
D_FF = 5632
D_IN = 3 * D_NA + D_F
EPS = 1e-6
N_DR = 2 * KH - 1
N_DC = 2 * KW - 1

N_TOK = BATCH * SEQ
assert N_TOK == DEC_BATCH * DEC_SEQ

V7X_LANES = 128
V7X_VMEM_LIMIT = 56 * 1024 * 1024

TM = 1024
N_MT = N_TOK // TM
NEG_INF = float("-inf")
CTX, LAT = "ctx", "lat"


def _params(n_axes):
    return pltpu.CompilerParams(dimension_semantics=("arbitrary",) * n_axes,
                                vmem_limit_bytes=V7X_VMEM_LIMIT)


def _mod_row(path, m, tile):
    return 0 if path == CTX else 1 + m // (DEC_SEQ // tile)


def _mod_spec(layer, path, chunk, width, col_map):
    return pl.BlockSpec((None, None, None, 1, width),
                        lambda m, s: (layer, _mod_row(path, m, TM), chunk, 0, col_map(s)))


def _tile_rows_spec(width, last_use):
    def index(m, s):
        return (jnp.minimum(m + jnp.where(s > last_use, 1, 0), N_MT - 1), 0)
    return pl.BlockSpec((TM, width), index)


def _sigmoid(z):
    return 1.0 / (1.0 + jnp.exp(-z))


def _bdot(a, b):
    return jnp.dot(a, b, preferred_element_type=F32)


MOD_TN = 1024


def _mod_kernel(cond_ref, w_ref, b_ref, o_ref):
    cnd = cond_ref[...]
    s = (cnd * _sigmoid(cnd)).astype(BF16)
    o_ref[...] = _bdot(s, w_ref[...].astype(BF16)) + b_ref[...]


def _modulation(cond8, w_mod, b_mod):
    n_blk = 6 * D_MODEL // MOD_TN
    return pl.pallas_call(
        _mod_kernel,
        grid=(DEPTH, n_blk),
        in_specs=[
            pl.BlockSpec((8, D_MODEL), lambda l, n: (0, 0)),
            pl.BlockSpec((None, D_MODEL, MOD_TN), lambda l, n: (l, 0, n)),
            pl.BlockSpec((None, 1, MOD_TN), lambda l, n: (l, 0, n)),
        ],
        out_specs=pl.BlockSpec((None, 8, MOD_TN), lambda l, n: (l, 0, n)),
        out_shape=jax.ShapeDtypeStruct((DEPTH, 8, 6 * D_MODEL), F32),
        compiler_params=_params(2),
        name="modulation",
    )(cond8, w_mod, b_mod.reshape(DEPTH, 1, 6 * D_MODEL))


def _bias_onehots():
    width = GRID_W * V7X_LANES
    oh_l = np.zeros((N_DC + 1, width), np.float32)
    oh_r = np.zeros((N_DC + 1, width), np.float32)
    mask = np.full((1, width), -np.inf, np.float32)
    for c in range(GRID_W):
        ws = min(max(c - KW // 2, 0), GRID_W - KW)
        for kc in range(ws, ws + KW):
            e = kc - c + KW - 1
            oh_l[e, c * V7X_LANES + kc] = 1.0
            oh_r[e, c * V7X_LANES + GRID_W + kc] = 1.0
            mask[0, c * V7X_LANES + kc] = 0.0
            mask[0, c * V7X_LANES + GRID_W + kc] = 0.0
    mask_last = mask.copy().reshape(GRID_W, V7X_LANES)
    mask_last[:, GRID_W:] = -np.inf
    return oh_l, oh_r, mask, mask_last.reshape(1, width)


def _split3(x):
    hi = x.astype(BF16)
    r1 = x - hi.astype(F32)
    mid = r1.astype(BF16)
    lo = (r1 - mid.astype(F32)).astype(BF16)
    return hi, mid, lo


def _bias_kernel(r1_ref, r2_ref, sel_ref, ohl_ref, ohr_ref, mask_ref, maskl_ref, o_ref):
    acc = None
    for r_ref, oh_ref in ((r1_ref, ohl_ref), (r2_ref, ohr_ref)):
        oh = oh_ref[...]
        for piece in _split3(r_ref[...]):
            t = _bdot(piece, oh)
            acc = t if acc is None else acc + t
    mask = jnp.where(sel_ref[...] > 0.5, maskl_ref[...], mask_ref[...])
    o_ref[...] = acc + mask


def _bias_tables(rpb):
    oh_l, oh_r, mask, mask_last = _bias_onehots()
    rows = N_HEADS * N_DR
    width = GRID_W * V7X_LANES
    pad = jnp.zeros((DEPTH, N_HEADS, N_DR, 1), F32)
    r1 = jnp.concatenate([rpb, pad], axis=-1).reshape(DEPTH * rows, N_DC + 1)
    nxt = jnp.concatenate([rpb[:, :, 1:], jnp.zeros((DEPTH, N_HEADS, 1, N_DC), F32)], axis=2)
    r2 = jnp.concatenate([nxt, pad], axis=-1).reshape(DEPTH * rows, N_DC + 1)
    sel = np.zeros((DEPTH, N_HEADS, N_DR, 1), np.float32)
    sel[:, :, N_DR - 1] = 1.0
    sel = jnp.asarray(sel.reshape(DEPTH * rows, 1))
    const = lambda shape: pl.BlockSpec(shape, lambda l: (0, 0))
    out = pl.pallas_call(
        _bias_kernel,
        grid=(DEPTH,),
        in_specs=[
            pl.BlockSpec((rows, N_DC + 1), lambda l: (l, 0)),
            pl.BlockSpec((rows, N_DC + 1), lambda l: (l, 0)),
            pl.BlockSpec((rows, 1), lambda l: (l, 0)),
            const((N_DC + 1, width)), const((N_DC + 1, width)),
            const((1, width)), const((1, width)),
        ],
        out_specs=pl.BlockSpec((rows, width), lambda l: (l, 0)),
        out_shape=jax.ShapeDtypeStruct((DEPTH * rows, width), F32),
        compiler_params=_params(1),
        name="bias_tables",
    )(r1, r2, sel, jnp.asarray(oh_l).astype(BF16), jnp.asarray(oh_r).astype(BF16),
      jnp.asarray(mask), jnp.asarray(mask_last))
    return out.reshape(DEPTH, N_HEADS, N_DR, GRID_W, V7X_LANES)


NORM_RC = 128


def _norm_mod_rows(x_ref, g_ref, sc_ref, sh_ref, dst_ref, inv_ref, rows):
    def stats(i, carry):
        r0 = pl.multiple_of(i * NORM_RC, NORM_RC)
        x = x_ref[pl.ds(r0, NORM_RC), :]
        inv_ref[pl.ds(r0, NORM_RC), :] = lax.rsqrt(jnp.mean(x * x, axis=-1, keepdims=True) + EPS)
        return carry

    lax.fori_loop(0, rows // NORM_RC, stats, 0)
    gain = g_ref[...]
    scale1 = 1.0 + sc_ref[...]
    shift = sh_ref[...]

    def apply(i, carry):
        r0 = pl.multiple_of(i * NORM_RC, NORM_RC)
        y = x_ref[pl.ds(r0, NORM_RC), :] * inv_ref[pl.ds(r0, NORM_RC), :]
        dst_ref[pl.ds(r0, NORM_RC), :] = ((y * gain) * scale1 + shift).astype(BF16)
        return carry

    lax.fori_loop(0, rows // NORM_RC, apply, 0)


IN_TN = 512
N_IN_BLK = D_IN // IN_TN
IN_Q_BLKS = D_NA // IN_TN
IN_QK_BLKS = 2 * IN_Q_BLKS
IN_QKV_BLKS = 3 * IN_Q_BLKS
KV_TILE_B = TM // SEQ
QK_SS_W = 256


def _inproj_kernel(with_kv, x_ref, g_ref, sc_ref, sh_ref, w_ref, qkg_ref, ones_ref, cs_ref, *rest):
    if with_kv:
        h_ref, p_ref, ac_ref, as_ref, k_ref, v_ref, inv_s = rest[-7:]
    else:
        h_ref, p_ref, ac_ref, as_ref, inv_s = rest[-5:]
    n = pl.program_id(1)

    @pl.when(n == 0)
    def _():
        _norm_mod_rows(x_ref, g_ref, sc_ref, sh_ref, h_ref, inv_s, TM)

    y = _bdot(h_ref[...], w_ref[...].astype(BF16))

    @pl.when(n < IN_QK_BLKS)
    def _():
        y2 = (y * y).astype(BF16)
        ones = ones_ref[...]
        ss = jnp.concatenate([_bdot(y2[:, i * QK_SS_W:(i + 1) * QK_SS_W], ones)
                              for i in range(IN_TN // QK_SS_W)], axis=1)
        yn = y * lax.rsqrt(ss * (1.0 / HEAD_DIM) + EPS) * qkg_ref[...]
        p_ref[...] = yn.astype(BF16)
        if with_kv:
            @pl.when(n >= IN_Q_BLKS)
            def _():
                k_ref[...] = yn.reshape(KV_TILE_B, SEQ, IN_TN)

    @pl.when(jnp.logical_and(n >= IN_QK_BLKS, n < IN_QKV_BLKS))
    def _():
        p_ref[...] = y.astype(BF16)
        if with_kv:
            v_ref[...] = y.reshape(KV_TILE_B, SEQ, IN_TN)

    @pl.when(n >= IN_QKV_BLKS)
    def _():
        cs = cs_ref[...]
        for gg in range(IN_TN // F_GROUP):
            a = _bdot(y[:, gg * F_GROUP:(gg + 1) * F_GROUP].astype(BF16), cs)
            ac_ref[:, gg * F_GROUP:(gg + 1) * F_GROUP] = a[:, :F_GROUP].astype(BF16)
            as_ref[:, gg * F_GROUP:(gg + 1) * F_GROUP] = a[:, F_GROUP:].astype(BF16)


def _in_projection(layer, path, x, mod, norm_g, w_in, qkg, ones_bd, cs, kv_prev=None):
    with_kv = path == CTX
    u_blk = lambda n: jnp.clip(n - IN_QKV_BLKS, 0, 1)
    in_specs = [
        _tile_rows_spec(D_MODEL, 0),
        pl.BlockSpec((None, 1, D_MODEL), lambda m, n: (layer, 0, 0)),
        _mod_spec(layer, path, 1, D_MODEL, lambda n: 0),
        _mod_spec(layer, path, 0, D_MODEL, lambda n: 0),
        pl.BlockSpec((None, D_MODEL, IN_TN), lambda m, n: (layer, 0, n)),
        pl.BlockSpec((None, None, 1, IN_TN),
                     lambda m, n: (layer, jnp.clip(n // IN_Q_BLKS, 0, 1), 0, 0)),
        pl.BlockSpec((QK_SS_W, QK_SS_W), lambda m, n: (0, 0)),
        pl.BlockSpec((F_GROUP, 2 * F_GROUP), lambda m, n: (0, 0)),
    ]
    args = [x, norm_g.reshape(DEPTH, 1, D_MODEL), mod, mod, w_in, qkg, ones_bd, cs]
    out_specs = [
        pl.BlockSpec((TM, D_MODEL), lambda m, n: (m, 0)),
        pl.BlockSpec((TM, IN_TN), lambda m, n: (m, jnp.minimum(n, IN_QKV_BLKS - 1))),
        pl.BlockSpec((TM, IN_TN), lambda m, n: (m, u_blk(n))),
        pl.BlockSpec((TM, IN_TN), lambda m, n: (m, u_blk(n))),
    ]
    out_shape = [
        jax.ShapeDtypeStruct((N_TOK, D_MODEL), BF16),
        jax.ShapeDtypeStruct((N_TOK, 3 * D_NA), BF16),
        jax.ShapeDtypeStruct((N_TOK, D_F), BF16),
        jax.ShapeDtypeStruct((N_TOK, D_F), BF16),
    ]
    aliases = {}
    if with_kv:
        kv_spec = lambda first: pl.BlockSpec(
            (KV_TILE_B, None, SEQ, IN_TN),
            lambda m, n: (m, layer, 0, jnp.clip(n - first, 0, IN_Q_BLKS - 1)))
        out_specs += [kv_spec(IN_Q_BLKS), kv_spec(IN_QK_BLKS)]
        out_shape += [jax.ShapeDtypeStruct((BATCH, DEPTH, SEQ, D_NA), F32)] * 2
        in_specs += [pl.BlockSpec(memory_space=pl.ANY)] * 2
        aliases = {len(args): 4, len(args) + 1: 5}
        args += list(kv_prev)

    def body(*refs):
        _inproj_kernel(with_kv, *refs)

    return pl.pallas_call(
        body,
        grid=(N_MT, N_IN_BLK),
        in_specs=in_specs,
        out_specs=out_specs,
        out_shape=out_shape,
        scratch_shapes=[pltpu.VMEM((TM, 1), F32)],
        input_output_aliases=aliases,
        compiler_params=_params(2),
        name=f"in_projection_{path}_l{layer}",
    )(*args)


def _stack_heads(q):
    lane = lax.broadcasted_iota(jnp.int32, q.shape, 1)
    zero = jnp.zeros_like(q)
    return jnp.concatenate([jnp.where(lane < HEAD_DIM, q, zero),
                            jnp.where(lane >= HEAD_DIM, q, zero)], axis=0)


def _unstack_heads(o2):
    rows = o2.shape[0] // 2
    lane = lax.broadcasted_iota(jnp.int32, (rows, V7X_LANES), 1)
    return jnp.where(lane < HEAD_DIM, o2[:rows], o2[rows:])


def _qk(q2, k):
    return lax.dot_general(q2, k, (((1,), (1,)), ((), ())), preferred_element_type=F32)


def _softmax(scores):
    mx = scores[0].max(axis=-1, keepdims=True)
    for s in scores[1:]:
        mx = jnp.maximum(mx, s.max(axis=-1, keepdims=True))
    exps = [jnp.exp(s - mx) for s in scores]
    den = exps[0].sum(axis=-1, keepdims=True)
    for e in exps[1:]:
        den = den + e.sum(axis=-1, keepdims=True)
    inv = 1.0 / den
    return [(e * inv).astype(BF16) for e in exps]


CTX_STEP_B = 2


def _ctx_attn_kernel(q_ref, k_ref, v_ref, o_ref):
    for b in range(CTX_STEP_B):
        rows = slice(b * SEQ, (b + 1) * SEQ)
        for p in range(N_HEADS // 2):
            cols = slice(p * V7X_LANES, (p + 1) * V7X_LANES)
            q2 = _stack_heads(q_ref[rows, cols])
            prob, = _softmax([_qk(q2, k_ref[rows, cols])])
            o2 = _bdot(prob, v_ref[rows, cols])
            o_ref[rows, cols] = _unstack_heads(o2).astype(BF16)


def _context_attention(layer, p):
    blk = CTX_STEP_B * SEQ
    return pl.pallas_call(
        _ctx_attn_kernel,
        grid=(BATCH // CTX_STEP_B,),
        in_specs=[pl.BlockSpec((blk, D_NA), lambda b: (b, 0)),
                  pl.BlockSpec((blk, D_NA), lambda b: (b, 1)),
                  pl.BlockSpec((blk, D_NA), lambda b: (b, 2))],
        out_specs=pl.BlockSpec((blk, D_NA), lambda b: (b, 0)),
        out_shape=jax.ShapeDtypeStruct((N_TOK, D_NA), BF16),
        compiler_params=_params(1),
        name=f"context_attention_l{layer}",
    )(p, p, p)


NA_QROWS = 4
NA_KROWS = 12
NA_KSTART = (0, 0, 4, 4)


def _na_bias_strip(tbl_ref, group, strip):
    hh, rr = divmod(strip, NA_QROWS)
    ks = NA_KSTART[group]
    lane = lax.broadcasted_iota(jnp.int32, (GRID_W, V7X_LANES), 1)
    ninf = jnp.full((GRID_W, V7X_LANES), NEG_INF, F32)
    r = NA_QROWS * group + rr
    rs = min(max(r - KH // 2, 0), ROWS - KH)
    tiles = []
    for a in range(NA_KROWS // 2):
        kk0 = ks + 2 * a
        ok0 = rs <= kk0 < rs + KH
        ok1 = rs <= kk0 + 1 < rs + KH
        dr0 = kk0 - r + KH - 1
        if ok0 and ok1:
            t = tbl_ref[hh, dr0]
        elif ok0:
            t = jnp.where(lane < GRID_W, tbl_ref[hh, dr0], ninf)
        elif ok1:
            t = jnp.where(lane >= GRID_W, tbl_ref[hh, dr0], ninf)
        else:
            t = ninf
        tiles.append(t)
    return jnp.concatenate(tiles, axis=1)


NA_PAIRS = 4
NA_LANES = NA_PAIRS * V7X_LANES


def _lat_attn_kernel(q_ref, k_ref, v_ref, kc_ref, vc_ref, tbl_ref, o_ref):
    qrows = NA_QROWS * GRID_W
    for pr in range(NA_PAIRS):
        cols = slice(pr * V7X_LANES, (pr + 1) * V7X_LANES)
        kc = kc_ref[:, cols].astype(BF16)
        vc = vc_ref[:, cols].astype(BF16)
        tbl = tbl_ref.at[2 * pr:2 * pr + 2]
        for g in range(ROWS // NA_QROWS):
            k0 = NA_KSTART[g] * GRID_W
            q2 = _stack_heads(q_ref[g * qrows:(g + 1) * qrows, cols])
            bias = jnp.concatenate([_na_bias_strip(tbl, g, t) for t in range(2 * NA_QROWS)], axis=0)
            s_loc = _qk(q2, k_ref[k0:k0 + NA_KROWS * GRID_W, cols]) + bias
            p_loc, p_ctx = _softmax([s_loc, _qk(q2, kc)])
            o2 = _bdot(p_loc, v_ref[k0:k0 + NA_KROWS * GRID_W, cols]) + _bdot(p_ctx, vc)
            o_ref[g * qrows:(g + 1) * qrows, cols] = _unstack_heads(o2).astype(BF16)


def _latent_attention(layer, p, cache_k, cache_v, tbl):
    n_blk = D_NA // NA_LANES
    cache_spec = pl.BlockSpec((None, None, PAST_LEN, NA_LANES), lambda b, h: (b, layer, 0, h))
    return pl.pallas_call(
        _lat_attn_kernel,
        grid=(DEC_BATCH, n_blk),
        in_specs=[
            pl.BlockSpec((DEC_SEQ, NA_LANES), lambda b, h: (b, h)),
            pl.BlockSpec((DEC_SEQ, NA_LANES), lambda b, h: (b, n_blk + h)),
            pl.BlockSpec((DEC_SEQ, NA_LANES), lambda b, h: (b, 2 * n_blk + h)),
            cache_spec, cache_spec,
            pl.BlockSpec((None, 2 * NA_PAIRS, N_DR, GRID_W, V7X_LANES), lambda b, h: (layer, h, 0, 0, 0)),
        ],
        out_specs=pl.BlockSpec((DEC_SEQ, NA_LANES), lambda b, h: (b, h)),
        out_shape=jax.ShapeDtypeStruct((N_TOK, D_NA), BF16),
        compiler_params=_params(2),
        name=f"latent_attention_l{layer}",
    )(p, p, p, cache_k, cache_v, tbl)


def _dft_tables(n, scale):
    j = np.arange(n, dtype=np.int64)
    ang = 2.0 * np.pi * ((j[:, None] * j[None, :]) % n).astype(np.float64) / n
    return (np.cos(ang) * scale).astype(np.float32), (np.sin(ang) * scale).astype(np.float32)


DFT_ROWS = 1024


def _pos_dft_kernel(seq, ct_ref, st_ref, ac_ref, as_ref, o_ref):
    ct = ct_ref[...]
    st = st_ref[...]
    for i in range(DFT_ROWS // seq):
        rows = slice(i * seq, (i + 1) * seq)
        o_ref[rows, :] = (_bdot(ct, ac_ref[rows, :]) - _bdot(st, as_ref[rows, :])).astype(BF16)


def _position_dft(layer, seq, a_cos, a_sin):
    ct, st = _dft_tables(seq, seq ** -0.5)
    blk = pl.BlockSpec((DFT_ROWS, D_F), lambda b: (b, 0))
    tab = pl.BlockSpec((seq, seq), lambda b: (0, 0))

    def body(*refs):
        _pos_dft_kernel(seq, *refs)

    return pl.pallas_call(
        body,
        grid=(N_TOK // DFT_ROWS,),
        in_specs=[tab, tab, blk, blk],
        out_specs=blk,
        out_shape=jax.ShapeDtypeStruct((N_TOK, D_F), BF16),
        compiler_params=_params(1),
        name=f"position_dft_{seq}_l{layer}",
    )(jnp.asarray(ct).astype(BF16), jnp.asarray(st).astype(BF16), a_cos, a_sin)


MIX_TC = 256
N_MIX_BLK = D_MODEL // MIX_TC
OUT_TN = 256
N_OUT_BLK = D_MODEL // OUT_TN


def _mix_kernel(h_ref, a_ref, f_ref, wgn_ref, wgf_ref, wna_ref, wf_ref, wo_ref, x_ref, g1_ref,
                o_ref, mix_s, wo_s):
    m = pl.program_id(0)
    s = pl.program_id(1)

    @pl.when(s < N_MIX_BLK)
    def _():
        h = h_ref[...]
        g_na = _sigmoid(_bdot(h, wgn_ref[...].astype(BF16)))
        g_fn = _sigmoid(_bdot(h, wgf_ref[...].astype(BF16)))
        na = _bdot(a_ref[...], wna_ref[...].astype(BF16))
        fn = _bdot(f_ref[...], wf_ref[...].astype(BF16))
        mix_s[s] = (g_na * na + g_fn * fn).astype(BF16)

    @pl.when(s >= N_MIX_BLK)
    def _():
        n = s - N_MIX_BLK

        @pl.when(m == 0)
        def _():
            wo_s[n] = wo_ref[...].astype(BF16)

        acc = None
        for c in range(N_MIX_BLK):
            t = _bdot(mix_s[c], wo_s[n, c * MIX_TC:(c + 1) * MIX_TC, :])
            acc = t if acc is None else acc + t
        o_ref[...] = x_ref[...] + g1_ref[...] * acc


def _token_mixing(layer, path, h, attn, f, x, mod, w_gate, w_na, w_f, w_o):
    mix_blk = lambda s: jnp.minimum(s, N_MIX_BLK - 1)
    out_blk = lambda s: jnp.clip(s - N_MIX_BLK, 0, N_OUT_BLK - 1)
    return pl.pallas_call(
        _mix_kernel,
        grid=(N_MT, N_MIX_BLK + N_OUT_BLK),
        in_specs=[
            _tile_rows_spec(D_MODEL, N_MIX_BLK - 1),
            _tile_rows_spec(D_NA, N_MIX_BLK - 1),
            _tile_rows_spec(D_F, N_MIX_BLK - 1),
            pl.BlockSpec((None, D_MODEL, MIX_TC), lambda m, s: (layer, 0, mix_blk(s))),
            pl.BlockSpec((None, D_MODEL, MIX_TC), lambda m, s: (layer, 0, N_MIX_BLK + mix_blk(s))),
            pl.BlockSpec((None, D_NA, MIX_TC), lambda m, s: (layer, 0, mix_blk(s))),
            pl.BlockSpec((None, D_F, MIX_TC), lambda m, s: (layer, 0, mix_blk(s))),
            pl.BlockSpec((None, D_MODEL, OUT_TN),
                         lambda m, s: (layer, 0, jnp.where(m == 0, out_blk(s), N_OUT_BLK - 1))),
            pl.BlockSpec((TM, OUT_TN), lambda m, s: (m, out_blk(s))),
            _mod_spec(layer, path, 2, OUT_TN, out_blk),
        ],
        out_specs=pl.BlockSpec((TM, OUT_TN), lambda m, s: (m, out_blk(s))),
        out_shape=jax.ShapeDtypeStruct((N_TOK, D_MODEL), F32),
        scratch_shapes=[pltpu.VMEM((N_MIX_BLK, TM, MIX_TC), BF16),
                        pltpu.VMEM((N_OUT_BLK, D_MODEL, OUT_TN), BF16)],
        compiler_params=_params(2),
        name=f"token_mixing_{path}_l{layer}",
    )(h, attn, f, w_gate, w_gate, w_na, w_f, w_o, x, mod)


FF_TC = 256
N_FF_BLK = D_FF // FF_TC
DOWN_TN = 256
N_DOWN_BLK = D_MODEL // DOWN_TN


def _ffn_kernel(xf_ref, ng_ref, sc_ref, sh_ref, wa_ref, wg_ref, wd_ref, x_ref, g2_ref, o_ref,
                act_s, h_s, inv_s):
    s = pl.program_id(1)

    @pl.when(s == 0)
    def _():
        _norm_mod_rows(xf_ref, ng_ref, sc_ref, sh_ref, h_s, inv_s, TM)

    @pl.when(s < N_FF_BLK)
    def _():
        h = h_s[...]
        g = _bdot(h, wg_ref[...].astype(BF16))
        gate = g * _sigmoid(g)
        a = _bdot(h, wa_ref[...].astype(BF16))
        act_s[s] = (gate * a).astype(BF16)

    @pl.when(s >= N_FF_BLK)
    def _():
        acc = None
        for j in range(N_FF_BLK):
            t = _bdot(act_s[j], wd_ref[j * FF_TC:(j + 1) * FF_TC, :].astype(BF16))
            acc = t if acc is None else acc + t
        o_ref[...] = x_ref[...] + g2_ref[...] * acc


def _ffn(layer, path, x, mod, norm_g, w_gate_up, w_down):
    ff_blk = lambda s: jnp.minimum(s, N_FF_BLK - 1)
    out_blk = lambda s: jnp.clip(s - N_FF_BLK, 0, N_DOWN_BLK - 1)
    return pl.pallas_call(
        _ffn_kernel,
        grid=(N_MT, N_FF_BLK + N_DOWN_BLK),
        in_specs=[
            _tile_rows_spec(D_MODEL, 0),
            pl.BlockSpec((None, 1, D_MODEL), lambda m, s: (layer, 0, 0)),
            _mod_spec(layer, path, 4, D_MODEL, lambda s: 0),
            _mod_spec(layer, path, 3, D_MODEL, lambda s: 0),
            pl.BlockSpec((None, D_MODEL, FF_TC), lambda m, s: (layer, 0, ff_blk(s))),
            pl.BlockSpec((None, D_MODEL, FF_TC), lambda m, s: (layer, 0, N_FF_BLK + ff_blk(s))),
            pl.BlockSpec((None, D_FF, DOWN_TN), lambda m, s: (layer, 0, out_blk(s))),
            pl.BlockSpec((TM, DOWN_TN), lambda m, s: (m, out_blk(s))),
            _mod_spec(layer, path, 5, DOWN_TN, out_blk),
        ],
        out_specs=pl.BlockSpec((TM, DOWN_TN), lambda m, s: (m, out_blk(s))),
        out_shape=jax.ShapeDtypeStruct((N_TOK, D_MODEL), F32),
        scratch_shapes=[pltpu.VMEM((N_FF_BLK, TM, FF_TC), BF16),
                        pltpu.VMEM((TM, D_MODEL), BF16),
                        pltpu.VMEM((TM, 1), F32)],
        compiler_params=_params(2),
        name=f"ffn_{path}_l{layer}",
    )(x, norm_g.reshape(DEPTH, 1, D_MODEL), mod, mod, w_gate_up, w_gate_up, w_down, x, mod)


def kernel(x_prompt, x_sample, cache_k, cache_v, c, c_ctx, w_mod, b_mod, norm1_g, norm2_g,
           w_in, q_norm_g, k_norm_g, rpb, w_na_proj, w_fnet_proj, w_gate, w_o, w_gate_up, w_down):
    xs = {CTX: x_prompt.reshape(N_TOK, D_MODEL), LAT: x_sample.reshape(N_TOK, D_MODEL)}
    cond8 = jnp.concatenate([c_ctx[None, :], c, jnp.zeros((8 - 1 - DEC_BATCH, D_MODEL), F32)], axis=0)
    mod = _modulation(cond8, w_mod, b_mod).reshape(DEPTH, 8, 6, 1, D_MODEL)
    tbl = _bias_tables(rpb)

    reps = IN_TN // HEAD_DIM
    qkg = jnp.stack([jnp.tile(q_norm_g * (HEAD_DIM ** -0.5), (1, reps)),
                     jnp.tile(k_norm_g, (1, reps))], axis=1).reshape(DEPTH, 2, 1, IN_TN)
    head_id = np.arange(QK_SS_W) // HEAD_DIM
    ones_bd = jnp.asarray((head_id[:, None] == head_id[None, :]).astype(np.float32)).astype(BF16)
    cc, sc = _dft_tables(F_GROUP, F_GROUP ** -0.5)
    cs = jnp.asarray(np.concatenate([cc, sc], axis=1)).astype(BF16)

    ck = cache_k.reshape(DEC_BATCH, DEPTH, PAST_LEN, D_NA)
    cv = cache_v.reshape(DEC_BATCH, DEPTH, PAST_LEN, D_NA)

    kv = tuple(jnp.zeros((BATCH, DEPTH, SEQ, D_NA), F32) for _ in range(2))
    for l in range(DEPTH):
        for path in (CTX, LAT):
            x = xs[path]
            outs = _in_projection(l, path, x, mod, norm1_g, w_in, qkg, ones_bd, cs, kv_prev=kv)
            h, p, a_cos, a_sin = outs[:4]
            if path == CTX:
                kv = outs[4:]
                attn = _context_attention(l, p)
                f = _position_dft(l, SEQ, a_cos, a_sin)
            else:
                attn = _latent_attention(l, p, ck, cv, tbl)
                f = _position_dft(l, DEC_SEQ, a_cos, a_sin)
            x1 = _token_mixing(l, path, h, attn, f, x, mod, w_gate, w_na_proj, w_fnet_proj, w_o)
            xs[path] = _ffn(l, path, x1, mod, norm2_g, w_gate_up, w_down)

    new_k, new_v = (t.reshape(BATCH, DEPTH, SEQ, N_HEADS, HEAD_DIM) for t in kv)
    return (xs[CTX].reshape(BATCH, SEQ, D_MODEL), xs[LAT].reshape(DEC_BATCH, DEC_SEQ, D_MODEL),
            new_k, new_v)
```

```python
import numpy as np
import jax
import jax.numpy as jnp
from jax import lax
from jax.experimental import pallas as pl
from jax.experimental.pallas import tpu as pltpu

F32 = jnp.float32
BF16 = jnp.bfloat16

D_MODEL = 2048
BATCH = 16
SEQ = 256
DEPTH = 2
DEC_BATCH = 4
DEC_SEQ = 1024
PAST_LEN = 256
GRID_W = 64
ROWS = DEC_SEQ // GRID_W
N_HEADS = 16
HEAD_DIM = 64
D_NA = N_HEADS * HEAD_DIM
D_F = D_MODEL // 2
N_FGROUPS = 4
F_GROUP = D_F // N_FGROUPS
KH = 8
KW = 16
D_FF = 5632
D_IN = 3 * D_NA + D_F
EPS = 1e-6
N_DR = 2 * KH - 1
N_DC = 2 * KW - 1

N_TOK = BATCH * SEQ
assert N_TOK == DEC_BATCH * DEC_SEQ

V7X_LANES = 128
V7X_VMEM_LIMIT = 56 * 1024 * 1024

TM = 1024
N_MT = N_TOK // TM
NEG_INF = float("-inf")
CTX, LAT = "ctx", "lat"


def _params(n_axes):
    return pltpu.CompilerParams(dimension_semantics=("arbitrary",) * n_axes,
                                vmem_limit_bytes=V7X_VMEM_LIMIT)


def _mod_row(path, m, tile):
    return 0 if path == CTX else 1 + m // (DEC_SEQ // tile)


def _mod_spec(layer, path, chunk, width, col_map):
    return pl.BlockSpec((None, None, None, 1, width),
                        lambda m, s: (layer, _mod_row(path, m, TM), chunk, 0, col_map(s)))


def _tile_rows_spec(width, last_use):
    def index(m, s):
        return (jnp.minimum(m + jnp.where(s > last_use, 1, 0), N_MT - 1), 0)
    return pl.BlockSpec((TM, width), index)


def _sigmoid(z):
    return 1.0 / (1.0 + jnp.exp(-z))


def _bdot(a, b):
    return jnp.dot(a, b, preferred_element_type=F32)


MOD_TN = 1024


def _mod_kernel(cond_ref, w_ref, b_ref, o_ref):
    cnd = cond_ref[...]
    s = (cnd * _sigmoid(cnd)).astype(BF16)
    o_ref[...] = _bdot(s, w_ref[...].astype(BF16)) + b_ref[...]


def _modulation(cond8, w_mod, b_mod):
    n_blk = 6 * D_MODEL // MOD_TN
    return pl.pallas_call(
        _mod_kernel,
        grid=(DEPTH, n_blk),
        in_specs=[
            pl.BlockSpec((8, D_MODEL), lambda l, n: (0, 0)),
            pl.BlockSpec((None, D_MODEL, MOD_TN), lambda l, n: (l, 0, n)),
            pl.BlockSpec((None, 1, MOD_TN), lambda l, n: (l, 0, n)),
        ],
        out_specs=pl.BlockSpec((None, 8, MOD_TN), lambda l, n: (l, 0, n)),
        out_shape=jax.ShapeDtypeStruct((DEPTH, 8, 6 * D_MODEL), F32),
        compiler_params=_params(2),
        name="modulation",
    )(cond8, w_mod, b_mod.reshape(DEPTH, 1, 6 * D_MODEL))


def _bias_onehots():
    width = GRID_W * V7X_LANES
    oh_l = np.zeros((N_DC + 1, width), np.float32)
    oh_r = np.zeros((N_DC + 1, width), np.float32)
    mask = np.full((1, width), -np.inf, np.float32)
    for c in range(GRID_W):
        ws = min(max(c - KW // 2, 0), GRID_W - KW)
        for kc in range(ws, ws + KW):
            e = kc - c + KW - 1
            oh_l[e, c * V7X_LANES + kc] = 1.0
            oh_r[e, c * V7X_LANES + GRID_W + kc] = 1.0
            mask[0, c * V7X_LANES + kc] = 0.0
            mask[0, c * V7X_LANES + GRID_W + kc] = 0.0
    mask_last = mask.copy().reshape(GRID_W, V7X_LANES)
    mask_last[:, GRID_W:] = -np.inf
    return oh_l, oh_r, mask, mask_last.reshape(1, width)


def _split3(x):
    hi = x.astype(BF16)
    r1 = x - hi.astype(F32)
    mid = r1.astype(BF16)
    lo = (r1 - mid.astype(F32)).astype(BF16)
    return hi, mid, lo


def _bias_kernel(r1_ref, r2_ref, sel_ref, ohl_ref, ohr_ref, mask_ref, maskl_ref, o_ref):
    acc = None
    for r_ref, oh_ref in ((r1_ref, ohl_ref), (r2_ref, ohr_ref)):
        oh = oh_ref[...]
        for piece in _split3(r_ref[...]):
            t = _bdot(piece, oh)
            acc = t if acc is None else acc + t
    mask = jnp.where(sel_ref[...] > 0.5, maskl_ref[...], mask_ref[...])
    o_ref[...] = acc + mask


def _bias_tables(rpb):
    oh_l, oh_r, mask, mask_last = _bias_onehots()
    rows = N_HEADS * N_DR
    width = GRID_W * V7X_LANES
    pad = jnp.zeros((DEPTH, N_HEADS, N_DR, 1), F32)
    r1 = jnp.concatenate([rpb, pad], axis=-1).reshape(DEPTH * rows, N_DC + 1)
    nxt = jnp.concatenate([rpb[:, :, 1:], jnp.zeros((DEPTH, N_HEADS, 1, N_DC), F32)], axis=2)
    r2 = jnp.concatenate([nxt, pad], axis=-1).reshape(DEPTH * rows, N_DC + 1)
    sel = np.zeros((DEPTH, N_HEADS, N_DR, 1), np.float32)
    sel[:, :, N_DR - 1] = 1.0
    sel = jnp.asarray(sel.reshape(DEPTH * rows, 1))
    const = lambda shape: pl.BlockSpec(shape, lambda l: (0, 0))
    out = pl.pallas_call(
        _bias_kernel,
        grid=(DEPTH,),
        in_specs=[
            pl.BlockSpec((rows, N_DC + 1), lambda l: (l, 0)),
            pl.BlockSpec((rows, N_DC + 1), lambda l: (l, 0)),
            pl.BlockSpec((rows, 1), lambda l: (l, 0)),
            const((N_DC + 1, width)), const((N_DC + 1, width)),
            const((1, width)), const((1, width)),
        ],
        out_specs=pl.BlockSpec((rows, width), lambda l: (l, 0)),
        out_shape=jax.ShapeDtypeStruct((DEPTH * rows, width), F32),
        compiler_params=_params(1),
        name="bias_tables",
    )(r1, r2, sel, jnp.asarray(oh_l).astype(BF16), jnp.asarray(oh_r).astype(BF16),
      jnp.asarray(mask), jnp.asarray(mask_last))
    return out.reshape(DEPTH, N_HEADS, N_DR, GRID_W, V7X_LANES)


NORM_RC = 128


def _norm_mod_rows(x_ref, g_ref, sc_ref, sh_ref, dst_ref, inv_ref, rows):
    def stats(i, carry):
        r0 = pl.multiple_of(i * NORM_RC, NORM_RC)
        x = x_ref[pl.ds(r0, NORM_RC), :]
        inv_ref[pl.ds(r0, NORM_RC), :] = lax.rsqrt(jnp.mean(x * x, axis=-1, keepdims=True) + EPS)
        return carry

    lax.fori_loop(0, rows // NORM_RC, stats, 0, unroll=True)
    gain = g_ref[...]
    scale1 = 1.0 + sc_ref[...]
    shift = sh_ref[...]

    def apply(i, carry):
        r0 = pl.multiple_of(i * NORM_RC, NORM_RC)
        y = x_ref[pl.ds(r0, NORM_RC), :] * inv_ref[pl.ds(r0, NORM_RC), :]
        dst_ref[pl.ds(r0, NORM_RC), :] = ((y * gain) * scale1 + shift).astype(BF16)
        return carry

    lax.fori_loop(0, rows // NORM_RC, apply, 0, unroll=True)


IN_TN = 512
N_IN_BLK = D_IN // IN_TN
IN_Q_BLKS = D_NA // IN_TN
IN_QK_BLKS = 2 * IN_Q_BLKS
IN_QKV_BLKS = 3 * IN_Q_BLKS
KV_TILE_B = TM // SEQ
QK_SS_W = 256


def _inproj_kernel(with_kv, x_ref, g_ref, sc_ref, sh_ref, w_ref, qkg_ref, ones_ref, cs_ref, *rest):
    if with_kv:
        h_ref, p_ref, ac_ref, as_ref, k_ref, v_ref, inv_s = rest[-7:]
    else:
        h_ref, p_ref, ac_ref, as_ref, inv_s = rest[-5:]
    n = pl.program_id(1)

    @pl.when(n == 0)
    def _():
        _norm_mod_rows(x_ref, g_ref, sc_ref, sh_ref, h_ref, inv_s, TM)

    y = _bdot(h_ref[...], w_ref[...].astype(BF16))

    @pl.when(n < IN_QK_BLKS)
    def _():
        y2 = (y * y).astype(BF16)
        ones = ones_ref[...]
        ss = jnp.concatenate([_bdot(y2[:, i * QK_SS_W:(i + 1) * QK_SS_W], ones)
                              for i in range(IN_TN // QK_SS_W)], axis=1)
        yn = y * lax.rsqrt(ss * (1.0 / HEAD_DIM) + EPS) * qkg_ref[...]
        p_ref[...] = yn.astype(BF16)
        if with_kv:
            @pl.when(n >= IN_Q_BLKS)
            def _():
                k_ref[...] = yn.reshape(KV_TILE_B, SEQ, IN_TN)

    @pl.when(jnp.logical_and(n >= IN_QK_BLKS, n < IN_QKV_BLKS))
    def _():
        p_ref[...] = y.astype(BF16)
        if with_kv:
            v_ref[...] = y.reshape(KV_TILE_B, SEQ, IN_TN)

    @pl.when(n >= IN_QKV_BLKS)
    def _():
        cs = cs_ref[...]
        for gg in range(IN_TN // F_GROUP):
            a = _bdot(y[:, gg * F_GROUP:(gg + 1) * F_GROUP].astype(BF16), cs)
            ac_ref[:, gg * F_GROUP:(gg + 1) * F_GROUP] = a[:, :F_GROUP].astype(BF16)
            as_ref[:, gg * F_GROUP:(gg + 1) * F_GROUP] = a[:, F_GROUP:].astype(BF16)


def _in_projection(layer, path, x, mod, norm_g, w_in, qkg, ones_bd, cs, kv_prev=None):
    with_kv = path == CTX
    u_blk = lambda n: jnp.clip(n - IN_QKV_BLKS, 0, 1)
    in_specs = [
        _tile_rows_spec(D_MODEL, 0),
        pl.BlockSpec((None, 1, D_MODEL), lambda m, n: (layer, 0, 0)),
        _mod_spec(layer, path, 1, D_MODEL, lambda n: 0),
        _mod_spec(layer, path, 0, D_MODEL, lambda n: 0),
        pl.BlockSpec((None, D_MODEL, IN_TN), lambda m, n: (layer, 0, n)),
        pl.BlockSpec((None, None, 1, IN_TN),
                     lambda m, n: (layer, jnp.clip(n // IN_Q_BLKS, 0, 1), 0, 0)),
        pl.BlockSpec((QK_SS_W, QK_SS_W), lambda m, n: (0, 0)),
        pl.BlockSpec((F_GROUP, 2 * F_GROUP), lambda m, n: (0, 0)),
    ]
    args = [x, norm_g.reshape(DEPTH, 1, D_MODEL), mod, mod, w_in, qkg, ones_bd, cs]
    out_specs = [
        pl.BlockSpec((TM, D_MODEL), lambda m, n: (m, 0)),
        pl.BlockSpec((TM, IN_TN), lambda m, n: (m, jnp.minimum(n, IN_QKV_BLKS - 1))),
        pl.BlockSpec((TM, IN_TN), lambda m, n: (m, u_blk(n))),
        pl.BlockSpec((TM, IN_TN), lambda m, n: (m, u_blk(n))),
    ]
    out_shape = [
        jax.ShapeDtypeStruct((N_TOK, D_MODEL), BF16),
        jax.ShapeDtypeStruct((N_TOK, 3 * D_NA), BF16),
        jax.ShapeDtypeStruct((N_TOK, D_F), BF16),
        jax.ShapeDtypeStruct((N_TOK, D_F), BF16),
    ]
    aliases = {}
    if with_kv:
        kv_spec = lambda first: pl.BlockSpec(
            (KV_TILE_B, None, SEQ, IN_TN),
            lambda m, n: (m, layer, 0, jnp.clip(n - first, 0, IN_Q_BLKS - 1)))
        out_specs += [kv_spec(IN_Q_BLKS), kv_spec(IN_QK_BLKS)]
        out_shape += [jax.ShapeDtypeStruct((BATCH, DEPTH, SEQ, D_NA), F32)] * 2
        in_specs += [pl.BlockSpec(memory_space=pl.ANY)] * 2
        aliases = {len(args): 4, len(args) + 1: 5}
        args += list(kv_prev)

    def body(*refs):
        _inproj_kernel(with_kv, *refs)

    return pl.pallas_call(
        body,
        grid=(N_MT, N_IN_BLK),
        in_specs=in_specs,
        out_specs=out_specs,
        out_shape=out_shape,
        scratch_shapes=[pltpu.VMEM((TM, 1), F32)],
        input_output_aliases=aliases,
        compiler_params=_params(2),
        name=f"in_projection_{path}_l{layer}",
    )(*args)


def _stack_heads(q):
    lane = lax.broadcasted_iota(jnp.int32, q.shape, 1)
    zero = jnp.zeros_like(q)
    return jnp.concatenate([jnp.where(lane < HEAD_DIM, q, zero),
                            jnp.where(lane >= HEAD_DIM, q, zero)], axis=0)


def _unstack_heads(o2):
    rows = o2.shape[0] // 2
    lane = lax.broadcasted_iota(jnp.int32, (rows, V7X_LANES), 1)
    return jnp.where(lane < HEAD_DIM, o2[:rows], o2[rows:])


def _qk(q2, k):
    return lax.dot_general(q2, k, (((1,), (1,)), ((), ())), preferred_element_type=F32)


def _softmax(scores):
    mx = scores[0].max(axis=-1, keepdims=True)
    for s in scores[1:]:
        mx = jnp.maximum(mx, s.max(axis=-1, keepdims=True))
    exps = [jnp.exp(s - mx) for s in scores]
    den = exps[0].sum(axis=-1, keepdims=True)
    for e in exps[1:]:
        den = den + e.sum(axis=-1, keepdims=True)
    inv = 1.0 / den
    return [(e * inv).astype(BF16) for e in exps]


CTX_STEP_B = 4


def _ctx_attn_kernel(q_ref, k_ref, v_ref, o_ref):
    for b in range(CTX_STEP_B):
        rows = slice(b * SEQ, (b + 1) * SEQ)
        for p in range(N_HEADS // 2):
            cols = slice(p * V7X_LANES, (p + 1) * V7X_LANES)
            q2 = _stack_heads(q_ref[rows, cols])
            prob, = _softmax([_qk(q2, k_ref[rows, cols])])
            o2 = _bdot(prob, v_ref[rows, cols])
            o_ref[rows, cols] = _unstack_heads(o2).astype(BF16)


def _context_attention(layer, p):
    blk = CTX_STEP_B * SEQ
    return pl.pallas_call(
        _ctx_attn_kernel,
        grid=(BATCH // CTX_STEP_B,),
        in_specs=[pl.BlockSpec((blk, D_NA), lambda b: (b, 0)),
                  pl.BlockSpec((blk, D_NA), lambda b: (b, 1)),
                  pl.BlockSpec((blk, D_NA), lambda b: (b, 2))],
        out_specs=pl.BlockSpec((blk, D_NA), lambda b: (b, 0)),
        out_shape=jax.ShapeDtypeStruct((N_TOK, D_NA), BF16),
        compiler_params=_params(1),
        name=f"context_attention_l{layer}",
    )(p, p, p)


NA_QROWS = 4
NA_KROWS = 12
NA_KSTART = (0, 0, 4, 4)


def _na_bias_strip(tbl_ref, group, strip):
    hh, rr = divmod(strip, NA_QROWS)
    ks = NA_KSTART[group]
    lane = lax.broadcasted_iota(jnp.int32, (GRID_W, V7X_LANES), 1)
    ninf = jnp.full((GRID_W, V7X_LANES), NEG_INF, F32)
    r = NA_QROWS * group + rr
    rs = min(max(r - KH // 2, 0), ROWS - KH)
    tiles = []
    for a in range(NA_KROWS // 2):
        kk0 = ks + 2 * a
        ok0 = rs <= kk0 < rs + KH
        ok1 = rs <= kk0 + 1 < rs + KH
        dr0 = kk0 - r + KH - 1
        if ok0 and ok1:
            t = tbl_ref[hh, dr0]
        elif ok0:
            t = jnp.where(lane < GRID_W, tbl_ref[hh, dr0], ninf)
        elif ok1:
            t = jnp.where(lane >= GRID_W, tbl_ref[hh, dr0], ninf)
        else:
            t = ninf
        tiles.append(t)
    return jnp.concatenate(tiles, axis=1)


NA_PAIRS = 4
NA_LANES = NA_PAIRS * V7X_LANES


def _lat_attn_kernel(q_ref, k_ref, v_ref, kc_ref, vc_ref, tbl_ref, o_ref):
    qrows = NA_QROWS * GRID_W
    for pr in range(NA_PAIRS):
        cols = slice(pr * V7X_LANES, (pr + 1) * V7X_LANES)
        kc = kc_ref[:, cols].astype(BF16)
        vc = vc_ref[:, cols].astype(BF16)
        tbl = tbl_ref.at[2 * pr:2 * pr + 2]
        for g in range(ROWS // NA_QROWS):
            k0 = NA_KSTART[g] * GRID_W
            q2 = _stack_heads(q_ref[g * qrows:(g + 1) * qrows, cols])
            bias = jnp.concatenate([_na_bias_strip(tbl, g, t) for t in range(2 * NA_QROWS)], axis=0)
            s_loc = _qk(q2, k_ref[k0:k0 + NA_KROWS * GRID_W, cols]) + bias
            p_loc, p_ctx = _softmax([s_loc, _qk(q2, kc)])
            o2 = _bdot(p_loc, v_ref[k0:k0 + NA_KROWS * GRID_W, cols]) + _bdot(p_ctx, vc)
            o_ref[g * qrows:(g + 1) * qrows, cols] = _unstack_heads(o2).astype(BF16)


def _latent_attention(layer, p, cache_k, cache_v, tbl):
    n_blk = D_NA // NA_LANES
    cache_spec = pl.BlockSpec((None, None, PAST_LEN, NA_LANES), lambda b, h: (b, layer, 0, h))
    return pl.pallas_call(
        _lat_attn_kernel,
        grid=(DEC_BATCH, n_blk),
        in_specs=[
            pl.BlockSpec((DEC_SEQ, NA_LANES), lambda b, h: (b, h)),
            pl.BlockSpec((DEC_SEQ, NA_LANES), lambda b, h: (b, n_blk + h)),
            pl.BlockSpec((DEC_SEQ, NA_LANES), lambda b, h: (b, 2 * n_blk + h)),
            cache_spec, cache_spec,
            pl.BlockSpec((None, 2 * NA_PAIRS, N_DR, GRID_W, V7X_LANES), lambda b, h: (layer, h, 0, 0, 0)),
        ],
        out_specs=pl.BlockSpec((DEC_SEQ, NA_LANES), lambda b, h: (b, h)),
        out_shape=jax.ShapeDtypeStruct((N_TOK, D_NA), BF16),
        compiler_params=_params(2),
        name=f"latent_attention_l{layer}",
    )(p, p, p, cache_k, cache_v, tbl)


def _dft_tables(n, scale):
    j = np.arange(n, dtype=np.int64)
    ang = 2.0 * np.pi * ((j[:, None] * j[None, :]) % n).astype(np.float64) / n
    return (np.cos(ang) * scale).astype(np.float32), (np.sin(ang) * scale).astype(np.float32)


DFT_ROWS = 1024


def _pos_dft_kernel(seq, ct_ref, st_ref, ac_ref, as_ref, o_ref):
    ct = ct_ref[...]
    st = st_ref[...]
    for i in range(DFT_ROWS // seq):
        rows = slice(i * seq, (i + 1) * seq)
        o_ref[rows, :] = (_bdot(ct, ac_ref[rows, :]) - _bdot(st, as_ref[rows, :])).astype(BF16)


def _position_dft(layer, seq, a_cos, a_sin):
    ct, st = _dft_tables(seq, seq ** -0.5)
    blk = pl.BlockSpec((DFT_ROWS, D_F), lambda b: (b, 0))
    tab = pl.BlockSpec((seq, seq), lambda b: (0, 0))

    def body(*refs):
        _pos_dft_kernel(seq, *refs)

    return pl.pallas_call(
        body,
        grid=(N_TOK // DFT_ROWS,),
        in_specs=[tab, tab, blk, blk],
        out_specs=blk,
        out_shape=jax.ShapeDtypeStruct((N_TOK, D_F), BF16),
        compiler_params=_params(1),
        name=f"position_dft_{seq}_l{layer}",
    )(jnp.asarray(ct).astype(BF16), jnp.asarray(st).astype(BF16), a_cos, a_sin)


MIX_TC = 256
N_MIX_BLK = D_MODEL // MIX_TC
OUT_TN = 256
N_OUT_BLK = D_MODEL // OUT_TN


def _mix_kernel(h_ref, a_ref, f_ref, wgn_ref, wgf_ref, wna_ref, wf_ref, wo_ref, x_ref, g1_ref,
                o_ref, mix_s, wo_s):
    m = pl.program_id(0)
    s = pl.program_id(1)

    @pl.when(s < N_MIX_BLK)
    def _():
        h = h_ref[...]
        g_na = _sigmoid(_bdot(h, wgn_ref[...].astype(BF16)))
        g_fn = _sigmoid(_bdot(h, wgf_ref[...].astype(BF16)))
        na = _bdot(a_ref[...], wna_ref[...].astype(BF16))
        fn = _bdot(f_ref[...], wf_ref[...].astype(BF16))
        mix_s[s] = (g_na * na + g_fn * fn).astype(BF16)

    @pl.when(s >= N_MIX_BLK)
    def _():
        n = s - N_MIX_BLK

        @pl.when(m == 0)
        def _():
            wo_s[n] = wo_ref[...].astype(BF16)

        acc = None
        for c in range(N_MIX_BLK):
            t = _bdot(mix_s[c], wo_s[n, c * MIX_TC:(c + 1) * MIX_TC, :])
            acc = t if acc is None else acc + t
        o_ref[...] = x_ref[...] + g1_ref[...] * acc


def _token_mixing(layer, path, h, attn, f, x, mod, w_gate, w_na, w_f, w_o):
    mix_blk = lambda s: jnp.minimum(s, N_MIX_BLK - 1)
    out_blk = lambda s: jnp.clip(s - N_MIX_BLK, 0, N_OUT_BLK - 1)
    return pl.pallas_call(
        _mix_kernel,
        grid=(N_MT, N_MIX_BLK + N_OUT_BLK),
        in_specs=[
            _tile_rows_spec(D_MODEL, N_MIX_BLK - 1),
            _tile_rows_spec(D_NA, N_MIX_BLK - 1),
            _tile_rows_spec(D_F, N_MIX_BLK - 1),
            pl.BlockSpec((None, D_MODEL, MIX_TC), lambda m, s: (layer, 0, mix_blk(s))),
            pl.BlockSpec((None, D_MODEL, MIX_TC), lambda m, s: (layer, 0, N_MIX_BLK + mix_blk(s))),
            pl.BlockSpec((None, D_NA, MIX_TC), lambda m, s: (layer, 0, mix_blk(s))),
            pl.BlockSpec((None, D_F, MIX_TC), lambda m, s: (layer, 0, mix_blk(s))),
            pl.BlockSpec((None, D_MODEL, OUT_TN),
                         lambda m, s: (layer, 0, jnp.where(m == 0, out_blk(s), N_OUT_BLK - 1))),
            pl.BlockSpec((TM, OUT_TN), lambda m, s: (m, out_blk(s))),
            _mod_spec(layer, path, 2, OUT_TN, out_blk),
        ],
        out_specs=pl.BlockSpec((TM, OUT_TN), lambda m, s: (m, out_blk(s))),
        out_shape=jax.ShapeDtypeStruct((N_TOK, D_MODEL), F32),
        scratch_shapes=[pltpu.VMEM((N_MIX_BLK, TM, MIX_TC), BF16),
                        pltpu.VMEM((N_OUT_BLK, D_MODEL, OUT_TN), BF16)],
        compiler_params=_params(2),
        name=f"token_mixing_{path}_l{layer}",
    )(h, attn, f, w_gate, w_gate, w_na, w_f, w_o, x, mod)


FF_TC = 256
N_FF_BLK = D_FF // FF_TC
DOWN_TN = 256
N_DOWN_BLK = D_MODEL // DOWN_TN


def _ffn_kernel(xf_ref, ng_ref, sc_ref, sh_ref, wa_ref, wg_ref, wd_ref, x_ref, g2_ref, o_ref,
                act_s, h_s, inv_s):
    s = pl.program_id(1)

    @pl.when(s == 0)
    def _():
        _norm_mod_rows(xf_ref, ng_ref, sc_ref, sh_ref, h_s, inv_s, TM)

    @pl.when(s < N_FF_BLK)
    def _():
        h = h_s[...]
        g = _bdot(h, wg_ref[...].astype(BF16))
        gate = g * _sigmoid(g)
        a = _bdot(h, wa_ref[...].astype(BF16))
        act_s[s] = (gate * a).astype(BF16)

    @pl.when(s >= N_FF_BLK)
    def _():
        acc = None
        for j in range(N_FF_BLK):
            t = _bdot(act_s[j], wd_ref[j * FF_TC:(j + 1) * FF_TC, :].astype(BF16))
            acc = t if acc is None else acc + t
        o_ref[...] = x_ref[...] + g2_ref[...] * acc


def _ffn(layer, path, x, mod, norm_g, w_gate_up, w_down):
    ff_blk = lambda s: jnp.minimum(s, N_FF_BLK - 1)
    out_blk = lambda s: jnp.clip(s - N_FF_BLK, 0, N_DOWN_BLK - 1)
    return pl.pallas_call(
        _ffn_kernel,
        grid=(N_MT, N_FF_BLK + N_DOWN_BLK),
        in_specs=[
            _tile_rows_spec(D_MODEL, 0),
            pl.BlockSpec((None, 1, D_MODEL), lambda m, s: (layer, 0, 0)),
            _mod_spec(layer, path, 4, D_MODEL, lambda s: 0),
            _mod_spec(layer, path, 3, D_MODEL, lambda s: 0),
            pl.BlockSpec((None, D_MODEL, FF_TC), lambda m, s: (layer, 0, ff_blk(s))),
            pl.BlockSpec((None, D_MODEL, FF_TC), lambda m, s: (layer, 0, N_FF_BLK + ff_blk(s))),
            pl.BlockSpec((None, D_FF, DOWN_TN), lambda m, s: (layer, 0, out_blk(s))),
            pl.BlockSpec((TM, DOWN_TN), lambda m, s: (m, out_blk(s))),
            _mod_spec(layer, path, 5, DOWN_TN, out_blk),
        ],
        out_specs=pl.BlockSpec((TM, DOWN_TN), lambda m, s: (m, out_blk(s))),
        out_shape=jax.ShapeDtypeStruct((N_TOK, D_MODEL), F32),
        scratch_shapes=[pltpu.VMEM((N_FF_BLK, TM, FF_TC), BF16),
                        pltpu.VMEM((TM, D_MODEL), BF16),
                        pltpu.VMEM((TM, 1), F32)],
        compiler_params=_params(2),
        name=f"ffn_{path}_l{layer}",
    )(x, norm_g.reshape(DEPTH, 1, D_MODEL), mod, mod, w_gate_up, w_gate_up, w_down, x, mod)


def kernel(x_prompt, x_sample, cache_k, cache_v, c, c_ctx, w_mod, b_mod, norm1_g, norm2_g,
           w_in, q_norm_g, k_norm_g, rpb, w_na_proj, w_fnet_proj, w_gate, w_o, w_gate_up, w_down):
    xs = {CTX: x_prompt.reshape(N_TOK, D_MODEL), LAT: x_sample.reshape(N_TOK, D_MODEL)}
    cond8 = jnp.concatenate([c_ctx[None, :], c, jnp.zeros((8 - 1 - DEC_BATCH, D_MODEL), F32)], axis=0)
    mod = _modulation(cond8, w_mod, b_mod).reshape(DEPTH, 8, 6, 1, D_MODEL)
    tbl = _bias_tables(rpb)

    reps = IN_TN // HEAD_DIM
    qkg = jnp.stack([jnp.tile(q_norm_g * (HEAD_DIM ** -0.5), (1, reps)),
                     jnp.tile(k_norm_g, (1, reps))], axis=1).reshape(DEPTH, 2, 1, IN_TN)
    head_id = np.arange(QK_SS_W) // HEAD_DIM
    ones_bd = jnp.asarray((head_id[:, None] == head_id[None, :]).astype(np.float32)).astype(BF16)
    cc, sc = _dft_tables(F_GROUP, F_GROUP ** -0.5)
    cs = jnp.asarray(np.concatenate([cc, sc], axis=1)).astype(BF16)

    ck = cache_k.reshape(DEC_BATCH, DEPTH, PAST_LEN, D_NA)
    cv = cache_v.reshape(DEC_BATCH, DEPTH, PAST_LEN, D_NA)

    kv = tuple(jnp.zeros((BATCH, DEPTH, SEQ, D_NA), F32) for _ in range(2))
    for l in range(DEPTH):
        for path in (CTX, LAT):
            x = xs[path]
            outs = _in_projection(l, path, x, mod, norm1_g, w_in, qkg, ones_bd, cs, kv_prev=kv)
            h, p, a_cos, a_sin = outs[:4]
            if path == CTX:
                kv = outs[4:]
                attn = _context_attention(l, p)
                f = _position_dft(l, SEQ, a_cos, a_sin)
            else:
                attn = _latent_attention(l, p, ck, cv, tbl)
                f = _position_dft(l, DEC_SEQ, a_cos, a_sin)
            x1 = _token_mixing(l, path, h, attn, f, x, mod, w_gate, w_na_proj, w_fnet_proj, w_o)
            xs[path] = _ffn(l, path, x1, mod, norm2_g, w_gate_up, w_down)

    new_k, new_v = (t.reshape(BATCH, DEPTH, SEQ, N_HEADS, HEAD_DIM) for t in kv)
    return (xs[CTX].reshape(BATCH, SEQ, D_MODEL), xs[LAT].reshape(DEC_BATCH, DEC_SEQ, D_MODEL),
            new_k, new_v)
```

```python
import numpy as np
import jax
import jax.numpy as jnp
from jax import lax
from jax.experimental import pallas as pl
from jax.experimental.pallas import tpu as pltpu

F32 = jnp.float32
BF16 = jnp.bfloat16

D_MODEL = 2048
BATCH = 16
SEQ = 256
DEPTH = 2
DEC_BATCH = 4
DEC_SEQ = 1024
PAST_LEN = 256
GRID_W = 64
ROWS = DEC_SEQ // GRID_W
N_HEADS = 16
HEAD_DIM = 64
D_NA = N_HEADS * HEAD_DIM
D_F = D_MODEL // 2
N_FGROUPS = 4
F_GROUP = D_F // N_FGROUPS
KH = 8
KW = 16
D_FF = 5632
D_IN = 3 * D_NA + D_F
EPS = 1e-6
N_DR = 2 * KH - 1
N_DC = 2 * KW - 1

N_TOK = BATCH * SEQ
assert N_TOK == DEC_BATCH * DEC_SEQ

V7X_LANES = 128
V7X_VMEM_LIMIT = 56 * 1024 * 1024

TM = 1024
N_MT = N_TOK // TM
NEG_INF = float("-inf")
CTX, LAT = "ctx", "lat"


def _params(n_axes):
    return pltpu.CompilerParams(dimension_semantics=("arbitrary",) * n_axes,
                                vmem_limit_bytes=V7X_VMEM_LIMIT)


def _mod_row(path, m, tile):
    return 0 if path == CTX else 1 + m // (DEC_SEQ // tile)


def _mod_spec(layer, path, chunk, width, col_map):
    return pl.BlockSpec((None, None, None, 1, width),
                        lambda m, s: (layer, _mod_row(path, m, TM), chunk, 0, col_map(s)))


def _tile_rows_spec(width, last_use):
    def index(m, s):
        return (jnp.minimum(m + jnp.where(s > last_use, 1, 0), N_MT - 1), 0)
    return pl.BlockSpec((TM, width), index)


def _sigmoid(z):
    return 1.0 / (1.0 + jnp.exp(-z))


def _bdot(a, b):
    return jnp.dot(a, b, preferred_element_type=F32)


MOD_TN = 1024


def _mod_kernel(cond_ref, w_ref, b_ref, o_ref):
    cnd = cond_ref[...]
    s = (cnd * _sigmoid(cnd)).astype(BF16)
    o_ref[...] = _bdot(s, w_ref[...].astype(BF16)) + b_ref[...]


def _modulation(cond8, w_mod, b_mod):
    n_blk = 6 * D_MODEL // MOD_TN
    return pl.pallas_call(
        _mod_kernel,
        grid=(DEPTH, n_blk),
        in_specs=[
            pl.BlockSpec((8, D_MODEL), lambda l, n: (0, 0)),
            pl.BlockSpec((None, D_MODEL, MOD_TN), lambda l, n: (l, 0, n)),
            pl.BlockSpec((None, 1, MOD_TN), lambda l, n: (l, 0, n)),
        ],
        out_specs=pl.BlockSpec((None, 8, MOD_TN), lambda l, n: (l, 0, n)),
        out_shape=jax.ShapeDtypeStruct((DEPTH, 8, 6 * D_MODEL), F32),
        compiler_params=_params(2),
        name="modulation",
    )(cond8, w_mod, b_mod.reshape(DEPTH, 1, 6 * D_MODEL))


def _bias_onehots():
    width = GRID_W * V7X_LANES
    oh_l = np.zeros((N_DC + 1, width), np.float32)
    oh_r = np.zeros((N_DC + 1, width), np.float32)
    mask = np.full((1, width), -np.inf, np.float32)
    for c in range(GRID_W):
        ws = min(max(c - KW // 2, 0), GRID_W - KW)
        for kc in range(ws, ws + KW):
            e = kc - c + KW - 1
            oh_l[e, c * V7X_LANES + kc] = 1.0
            oh_r[e, c * V7X_LANES + GRID_W + kc] = 1.0
            mask[0, c * V7X_LANES + kc] = 0.0
            mask[0, c * V7X_LANES + GRID_W + kc] = 0.0
    mask_last = mask.copy().reshape(GRID_W, V7X_LANES)
    mask_last[:, GRID_W:] = -np.inf
    return oh_l, oh_r, mask, mask_last.reshape(1, width)


def _split3(x):
    hi = x.astype(BF16)
    r1 = x - hi.astype(F32)
    mid = r1.astype(BF16)
    lo = (r1 - mid.astype(F32)).astype(BF16)
    return hi, mid, lo


def _bias_kernel(r1_ref, r2_ref, sel_ref, ohl_ref, ohr_ref, mask_ref, maskl_ref, o_ref):
    acc = None
    for r_ref, oh_ref in ((r1_ref, ohl_ref), (r2_ref, ohr_ref)):
        oh = oh_ref[...]
        for piece in _split3(r_ref[...]):
            t = _bdot(piece, oh)
            acc = t if acc is None else acc + t
    mask = jnp.where(sel_ref[...] > 0.5, maskl_ref[...], mask_ref[...])
    o_ref[...] = (acc + mask).reshape(o_ref.shape)


def _bias_tables(rpb):
    oh_l, oh_r, mask, mask_last = _bias_onehots()
    rows = N_HEADS * N_DR
    width = GRID_W * V7X_LANES
    pad = jnp.zeros((DEPTH, N_HEADS, N_DR, 1), F32)
    r1 = jnp.concatenate([rpb, pad], axis=-1).reshape(DEPTH * rows, N_DC + 1)
    nxt = jnp.concatenate([rpb[:, :, 1:], jnp.zeros((DEPTH, N_HEADS, 1, N_DC), F32)], axis=2)
    r2 = jnp.concatenate([nxt, pad], axis=-1).reshape(DEPTH * rows, N_DC + 1)
    sel = np.zeros((DEPTH, N_HEADS, N_DR, 1), np.float32)
    sel[:, :, N_DR - 1] = 1.0
    sel = jnp.asarray(sel.reshape(DEPTH * rows, 1))
    const = lambda shape: pl.BlockSpec(shape, lambda l: (0, 0))
    out = pl.pallas_call(
        _bias_kernel,
        grid=(DEPTH,),
        in_specs=[
            pl.BlockSpec((rows, N_DC + 1), lambda l: (l, 0)),
            pl.BlockSpec((rows, N_DC + 1), lambda l: (l, 0)),
            pl.BlockSpec((rows, 1), lambda l: (l, 0)),
            const((N_DC + 1, width)), const((N_DC + 1, width)),
            const((1, width)), const((1, width)),
        ],
        out_specs=pl.BlockSpec((rows, GRID_W, V7X_LANES), lambda l: (l, 0, 0)),
        out_shape=jax.ShapeDtypeStruct((DEPTH * rows, GRID_W, V7X_LANES), F32),
        compiler_params=_params(1),
        name="bias_tables",
    )(r1, r2, sel, jnp.asarray(oh_l).astype(BF16), jnp.asarray(oh_r).astype(BF16),
      jnp.asarray(mask), jnp.asarray(mask_last))
    return out.reshape(DEPTH, N_HEADS, N_DR, GRID_W, V7X_LANES)


NORM_RC = 128


def _norm_mod_rows(x_ref, g_ref, sc_ref, sh_ref, dst_ref, inv_ref, rows):
    def stats(i, carry):
        r0 = pl.multiple_of(i * NORM_RC, NORM_RC)
        x = x_ref[pl.ds(r0, NORM_RC), :]
        inv_ref[pl.ds(r0, NORM_RC), :] = lax.rsqrt(jnp.mean(x * x, axis=-1, keepdims=True) + EPS)
        return carry

    lax.fori_loop(0, rows // NORM_RC, stats, 0, unroll=True)
    gain = g_ref[...]
    scale1 = 1.0 + sc_ref[...]
    shift = sh_ref[...]

    def apply(i, carry):
        r0 = pl.multiple_of(i * NORM_RC, NORM_RC)
        y = x_ref[pl.ds(r0, NORM_RC), :] * inv_ref[pl.ds(r0, NORM_RC), :]
        dst_ref[pl.ds(r0, NORM_RC), :] = ((y * gain) * scale1 + shift).astype(BF16)
        return carry

    lax.fori_loop(0, rows // NORM_RC, apply, 0, unroll=True)


IN_TN = 512
N_IN_BLK = D_IN // IN_TN
IN_Q_BLKS = D_NA // IN_TN
IN_QK_BLKS = 2 * IN_Q_BLKS
IN_QKV_BLKS = 3 * IN_Q_BLKS
KV_TILE_B = TM // SEQ
QK_SS_W = 256


def _inproj_kernel(with_kv, x_ref, g_ref, sc_ref, sh_ref, w_ref, qkg_ref, ones_ref, cs_ref, *rest):
    if with_kv:
        h_ref, p_ref, ac_ref, as_ref, k_ref, v_ref, inv_s = rest[-7:]
    else:
        h_ref, p_ref, ac_ref, as_ref, inv_s = rest[-5:]
    n = pl.program_id(1)

    def project():
        return _bdot(h_ref[...], w_ref[...].astype(BF16))

    def finish_qk(y):
        y2 = (y * y).astype(BF16)
        ones = ones_ref[...]
        ss = jnp.concatenate([_bdot(y2[:, i * QK_SS_W:(i + 1) * QK_SS_W], ones)
                              for i in range(IN_TN // QK_SS_W)], axis=1)
        yn = y * lax.rsqrt(ss * (1.0 / HEAD_DIM) + EPS) * qkg_ref[...]
        p_ref[...] = yn.astype(BF16)
        if with_kv:
            @pl.when(n >= IN_Q_BLKS)
            def _():
                k_ref[...] = yn.reshape(KV_TILE_B, SEQ, IN_TN)

    def finish_v(y):
        p_ref[...] = y.astype(BF16)
        if with_kv:
            v_ref[...] = y.reshape(KV_TILE_B, SEQ, IN_TN)

    def finish_u(y):
        cs = cs_ref[...]
        for gg in range(IN_TN // F_GROUP):
            a = _bdot(y[:, gg * F_GROUP:(gg + 1) * F_GROUP].astype(BF16), cs)
            ac_ref[:, gg * F_GROUP:(gg + 1) * F_GROUP] = a[:, :F_GROUP].astype(BF16)
            as_ref[:, gg * F_GROUP:(gg + 1) * F_GROUP] = a[:, F_GROUP:].astype(BF16)

    @pl.when(n == 0)
    def _():
        _norm_mod_rows(x_ref, g_ref, sc_ref, sh_ref, h_ref, inv_s, TM)
        finish_qk(project())

    @pl.when(n > 0)
    def _():
        y = project()

        @pl.when(n < IN_QK_BLKS)
        def _():
            finish_qk(y)

        @pl.when(jnp.logical_and(n >= IN_QK_BLKS, n < IN_QKV_BLKS))
        def _():
            finish_v(y)

        @pl.when(n >= IN_QKV_BLKS)
        def _():
            finish_u(y)


def _in_projection(layer, path, x, mod, norm_g, w_in, qkg, ones_bd, cs, kv_prev=None):
    with_kv = path == CTX
    u_blk = lambda n: jnp.clip(n - IN_QKV_BLKS, 0, 1)
    in_specs = [
        _tile_rows_spec(D_MODEL, 0),
        pl.BlockSpec((None, 1, D_MODEL), lambda m, n: (layer, 0, 0)),
        _mod_spec(layer, path, 1, D_MODEL, lambda n: 0),
        _mod_spec(layer, path, 0, D_MODEL, lambda n: 0),
        pl.BlockSpec((None, D_MODEL, IN_TN), lambda m, n: (layer, 0, n)),
        pl.BlockSpec((None, None, 1, IN_TN),
                     lambda m, n: (layer, jnp.clip(n // IN_Q_BLKS, 0, 1), 0, 0)),
        pl.BlockSpec((QK_SS_W, QK_SS_W), lambda m, n: (0, 0)),
        pl.BlockSpec((F_GROUP, 2 * F_GROUP), lambda m, n: (0, 0)),
    ]
    args = [x, norm_g.reshape(DEPTH, 1, D_MODEL), mod, mod, w_in, qkg, ones_bd, cs]
    out_specs = [
        pl.BlockSpec((TM, D_MODEL), lambda m, n: (m, 0)),
        pl.BlockSpec((TM, IN_TN), lambda m, n: (m, jnp.minimum(n, IN_QKV_BLKS - 1))),
        pl.BlockSpec((TM, IN_TN), lambda m, n: (m, u_blk(n))),
        pl.BlockSpec((TM, IN_TN), lambda m, n: (m, u_blk(n))),
    ]
    out_shape = [
        jax.ShapeDtypeStruct((N_TOK, D_MODEL), BF16),
        jax.ShapeDtypeStruct((N_TOK, 3 * D_NA), BF16),
        jax.ShapeDtypeStruct((N_TOK, D_F), BF16),
        jax.ShapeDtypeStruct((N_TOK, D_F), BF16),
    ]
    aliases = {}
    if with_kv:
        kv_spec = lambda first: pl.BlockSpec(
            (KV_TILE_B, None, SEQ, IN_TN),
            lambda m, n: (m, layer, 0, jnp.clip(n - first, 0, IN_Q_BLKS - 1)))
        out_specs += [kv_spec(IN_Q_BLKS), kv_spec(IN_QK_BLKS)]
        out_shape += [jax.ShapeDtypeStruct((BATCH, DEPTH, SEQ, D_NA), F32)] * 2
        in_specs += [pl.BlockSpec(memory_space=pl.ANY)] * 2
        aliases = {len(args): 4, len(args) + 1: 5}
        args += list(kv_prev)

    def body(*refs):
        _inproj_kernel(with_kv, *refs)

    return pl.pallas_call(
        body,
        grid=(N_MT, N_IN_BLK),
        in_specs=in_specs,
        out_specs=out_specs,
        out_shape=out_shape,
        scratch_shapes=[pltpu.VMEM((TM, 1), F32)],
        input_output_aliases=aliases,
        compiler_params=_params(2),
        name=f"in_projection_{path}_l{layer}",
    )(*args)


def _stack_heads(q):
    lane = lax.broadcasted_iota(jnp.int32, q.shape, 1)
    zero = jnp.zeros_like(q)
    return jnp.concatenate([jnp.where(lane < HEAD_DIM, q, zero),
                            jnp.where(lane >= HEAD_DIM, q, zero)], axis=0)


def _unstack_heads(o2):
    rows = o2.shape[0] // 2
    lane = lax.broadcasted_iota(jnp.int32, (rows, V7X_LANES), 1)
    return jnp.where(lane < HEAD_DIM, o2[:rows], o2[rows:])


def _qk(q2, k):
    return lax.dot_general(q2, k, (((1,), (1,)), ((), ())), preferred_element_type=F32)


def _softmax(scores):
    mx = scores[0].max(axis=-1, keepdims=True)
    for s in scores[1:]:
        mx = jnp.maximum(mx, s.max(axis=-1, keepdims=True))
    exps = [jnp.exp(s - mx) for s in scores]
    den = exps[0].sum(axis=-1, keepdims=True)
    for e in exps[1:]:
        den = den + e.sum(axis=-1, keepdims=True)
    inv = 1.0 / den
    return [(e * inv).astype(BF16) for e in exps]


CTX_STEP_B = 4


def _ctx_attn_kernel(q_ref, k_ref, v_ref, o_ref):
    for b in range(CTX_STEP_B):
        rows = slice(b * SEQ, (b + 1) * SEQ)
        for p in range(N_HEADS // 2):
            cols = slice(p * V7X_LANES, (p + 1) * V7X_LANES)
            q2 = _stack_heads(q_ref[rows, cols])
            prob, = _softmax([_qk(q2, k_ref[rows, cols])])
            o2 = _bdot(prob, v_ref[rows, cols])
            o_ref[rows, cols] = _unstack_heads(o2).astype(BF16)


def _context_attention(layer, p):
    blk = CTX_STEP_B * SEQ
    return pl.pallas_call(
        _ctx_attn_kernel,
        grid=(BATCH // CTX_STEP_B,),
        in_specs=[pl.BlockSpec((blk, D_NA), lambda b: (b, 0)),
                  pl.BlockSpec((blk, D_NA), lambda b: (b, 1)),
                  pl.BlockSpec((blk, D_NA), lambda b: (b, 2))],
        out_specs=pl.BlockSpec((blk, D_NA), lambda b: (b, 0)),
        out_shape=jax.ShapeDtypeStruct((N_TOK, D_NA), BF16),
        compiler_params=_params(1),
        name=f"context_attention_l{layer}",
    )(p, p, p)


NA_QROWS = 4
NA_KROWS = 12
NA_KSTART = (0, 0, 4, 4)


def _na_bias_strip(tbl_ref, group, strip):
    hh, rr = divmod(strip, NA_QROWS)
    ks = NA_KSTART[group]
    lane = lax.broadcasted_iota(jnp.int32, (GRID_W, V7X_LANES), 1)
    ninf = jnp.full((GRID_W, V7X_LANES), NEG_INF, F32)
    r = NA_QROWS * group + rr
    rs = min(max(r - KH // 2, 0), ROWS - KH)
    tiles = []
    for a in range(NA_KROWS // 2):
        kk0 = ks + 2 * a
        ok0 = rs <= kk0 < rs + KH
        ok1 = rs <= kk0 + 1 < rs + KH
        dr0 = kk0 - r + KH - 1
        if ok0 and ok1:
            t = tbl_ref[hh, dr0]
        elif ok0:
            t = jnp.where(lane < GRID_W, tbl_ref[hh, dr0], ninf)
        elif ok1:
            t = jnp.where(lane >= GRID_W, tbl_ref[hh, dr0], ninf)
        else:
            t = ninf
        tiles.append(t)
    return jnp.concatenate(tiles, axis=1)


NA_PAIRS = 4
NA_LANES = NA_PAIRS * V7X_LANES


def _lat_attn_kernel(q_ref, k_ref, v_ref, kc_ref, vc_ref, tbl_ref, o_ref):
    qrows = NA_QROWS * GRID_W
    for pr in range(NA_PAIRS):
        cols = slice(pr * V7X_LANES, (pr + 1) * V7X_LANES)
        kc = kc_ref[:, cols].astype(BF16)
        vc = vc_ref[:, cols].astype(BF16)
        tbl = tbl_ref.at[2 * pr:2 * pr + 2]
        for g in range(ROWS // NA_QROWS):
            k0 = NA_KSTART[g] * GRID_W
            q2 = _stack_heads(q_ref[g * qrows:(g + 1) * qrows, cols])
            bias = jnp.concatenate([_na_bias_strip(tbl, g, t) for t in range(2 * NA_QROWS)], axis=0)
            s_loc = _qk(q2, k_ref[k0:k0 + NA_KROWS * GRID_W, cols]) + bias
            p_loc, p_ctx = _softmax([s_loc, _qk(q2, kc)])
            o2 = _bdot(p_loc, v_ref[k0:k0 + NA_KROWS * GRID_W, cols]) + _bdot(p_ctx, vc)
            o_ref[g * qrows:(g + 1) * qrows, cols] = _unstack_heads(o2).astype(BF16)


def _latent_attention(layer, p, cache_k, cache_v, tbl):
    n_blk = D_NA // NA_LANES
    cache_spec = pl.BlockSpec((None, None, PAST_LEN, NA_LANES), lambda b, h: (b, layer, 0, h))
    return pl.pallas_call(
        _lat_attn_kernel,
        grid=(DEC_BATCH, n_blk),
        in_specs=[
            pl.BlockSpec((DEC_SEQ, NA_LANES), lambda b, h: (b, h)),
            pl.BlockSpec((DEC_SEQ, NA_LANES), lambda b, h: (b, n_blk + h)),
            pl.BlockSpec((DEC_SEQ, NA_LANES), lambda b, h: (b, 2 * n_blk + h)),
            cache_spec, cache_spec,
            pl.BlockSpec((None, 2 * NA_PAIRS, N_DR, GRID_W, V7X_LANES), lambda b, h: (layer, h, 0, 0, 0)),
        ],
        out_specs=pl.BlockSpec((DEC_SEQ, NA_LANES), lambda b, h: (b, h)),
        out_shape=jax.ShapeDtypeStruct((N_TOK, D_NA), BF16),
        compiler_params=_params(2),
        name=f"latent_attention_l{layer}",
    )(p, p, p, cache_k, cache_v, tbl)


def _dft_tables(n, scale):
    j = np.arange(n, dtype=np.int64)
    ang = 2.0 * np.pi * ((j[:, None] * j[None, :]) % n).astype(np.float64) / n
    return (np.cos(ang) * scale).astype(np.float32), (np.sin(ang) * scale).astype(np.float32)


DFT_ROWS = 1024


def _pos_dft_kernel(seq, ct_ref, st_ref, ac_ref, as_ref, o_ref):
    ct = ct_ref[...]
    st = st_ref[...]
    for i in range(DFT_ROWS // seq):
        rows = slice(i * seq, (i + 1) * seq)
        o_ref[rows, :] = (_bdot(ct, ac_ref[rows, :]) - _bdot(st, as_ref[rows, :])).astype(BF16)


def _position_dft(layer, seq, a_cos, a_sin):
    ct, st = _dft_tables(seq, seq ** -0.5)
    blk = pl.BlockSpec((DFT_ROWS, D_F), lambda b: (b, 0))
    tab = pl.BlockSpec((seq, seq), lambda b: (0, 0))

    def body(*refs):
        _pos_dft_kernel(seq, *refs)

    return pl.pallas_call(
        body,
        grid=(N_TOK // DFT_ROWS,),
        in_specs=[tab, tab, blk, blk],
        out_specs=blk,
        out_shape=jax.ShapeDtypeStruct((N_TOK, D_F), BF16),
        compiler_params=_params(1),
        name=f"position_dft_{seq}_l{layer}",
    )(jnp.asarray(ct).astype(BF16), jnp.asarray(st).astype(BF16), a_cos, a_sin)


MIX_TC = 256
N_MIX_BLK = D_MODEL // MIX_TC
OUT_TN = 256
N_OUT_BLK = D_MODEL // OUT_TN


def _mix_kernel(h_ref, a_ref, f_ref, wgn_ref, wgf_ref, wna_ref, wf_ref, wo_ref, x_ref, g1_ref,
                o_ref, mix_s, wo_s):
    m = pl.program_id(0)
    s = pl.program_id(1)

    @pl.when(s < N_MIX_BLK)
    def _():
        h = h_ref[...]
        g_na = _sigmoid(_bdot(h, wgn_ref[...].astype(BF16)))
        g_fn = _sigmoid(_bdot(h, wgf_ref[...].astype(BF16)))
        na = _bdot(a_ref[...], wna_ref[...].astype(BF16))
        fn = _bdot(f_ref[...], wf_ref[...].astype(BF16))
        mix_s[s] = (g_na * na + g_fn * fn).astype(BF16)

    @pl.when(s >= N_MIX_BLK)
    def _():
        n = s - N_MIX_BLK

        @pl.when(m == 0)
        def _():
            wo_s[n] = wo_ref[...].astype(BF16)

        acc = None
        for c in range(N_MIX_BLK):
            t = _bdot(mix_s[c], wo_s[n, c * MIX_TC:(c + 1) * MIX_TC, :])
            acc = t if acc is None else acc + t
        o_ref[...] = x_ref[...] + g1_ref[...] * acc


def _token_mixing(layer, path, h, attn, f, x, mod, w_gate, w_na, w_f, w_o):
    mix_blk = lambda s: jnp.minimum(s, N_MIX_BLK - 1)
    out_blk = lambda s: jnp.clip(s - N_MIX_BLK, 0, N_OUT_BLK - 1)
    return pl.pallas_call(
        _mix_kernel,
        grid=(N_MT, N_MIX_BLK + N_OUT_BLK),
        in_specs=[
            _tile_rows_spec(D_MODEL, N_MIX_BLK - 1),
            _tile_rows_spec(D_NA, N_MIX_BLK - 1),
            _tile_rows_spec(D_F, N_MIX_BLK - 1),
            pl.BlockSpec((None, D_MODEL, MIX_TC), lambda m, s: (layer, 0, mix_blk(s))),
            pl.BlockSpec((None, D_MODEL, MIX_TC), lambda m, s: (layer, 0, N_MIX_BLK + mix_blk(s))),
            pl.BlockSpec((None, D_NA, MIX_TC), lambda m, s: (layer, 0, mix_blk(s))),
            pl.BlockSpec((None, D_F, MIX_TC), lambda m, s: (layer, 0, mix_blk(s))),
            pl.BlockSpec((None, D_MODEL, OUT_TN),
                         lambda m, s: (layer, 0, jnp.where(m == 0, out_blk(s), N_OUT_BLK - 1))),
            pl.BlockSpec((TM, OUT_TN), lambda m, s: (m, out_blk(s))),
            _mod_spec(layer, path, 2, OUT_TN, out_blk),
        ],
        out_specs=pl.BlockSpec((TM, OUT_TN), lambda m, s: (m, out_blk(s))),
        out_shape=jax.ShapeDtypeStruct((N_TOK, D_MODEL), F32),
        scratch_shapes=[pltpu.VMEM((N_MIX_BLK, TM, MIX_TC), BF16),
                        pltpu.VMEM((N_OUT_BLK, D_MODEL, OUT_TN), BF16)],
        compiler_params=_params(2),
        name=f"token_mixing_{path}_l{layer}",
    )(h, attn, f, w_gate, w_gate, w_na, w_f, w_o, x, mod)


FF_TC = 256
N_FF_BLK = D_FF // FF_TC
DOWN_TN = 256
N_DOWN_BLK = D_MODEL // DOWN_TN


def _ffn_kernel(xf_ref, ng_ref, sc_ref, sh_ref, wa_ref, wg_ref, wd_ref, x_ref, g2_ref, o_ref,
                act_s, h_s, inv_s):
    s = pl.program_id(1)

    @pl.when(s == 0)
    def _():
        _norm_mod_rows(xf_ref, ng_ref, sc_ref, sh_ref, h_s, inv_s, TM)

    @pl.when(s < N_FF_BLK)
    def _():
        h = h_s[...]
        g = _bdot(h, wg_ref[...].astype(BF16))
        gate = g * _sigmoid(g)
        a = _bdot(h, wa_ref[...].astype(BF16))
        act_s[s] = (gate * a).astype(BF16)

    @pl.when(s >= N_FF_BLK)
    def _():
        acc = None
        for j in range(N_FF_BLK):
            t = _bdot(act_s[j], wd_ref[j * FF_TC:(j + 1) * FF_TC, :].astype(BF16))
            acc = t if acc is None else acc + t
        o_ref[...] = x_ref[...] + g2_ref[...] * acc


def _ffn(layer, path, x, mod, norm_g, w_gate_up, w_down):
    ff_blk = lambda s: jnp.minimum(s, N_FF_BLK - 1)
    out_blk = lambda s: jnp.clip(s - N_FF_BLK, 0, N_DOWN_BLK - 1)
    return pl.pallas_call(
        _ffn_kernel,
        grid=(N_MT, N_FF_BLK + N_DOWN_BLK),
        in_specs=[
            _tile_rows_spec(D_MODEL, 0),
            pl.BlockSpec((None, 1, D_MODEL), lambda m, s: (layer, 0, 0)),
            _mod_spec(layer, path, 4, D_MODEL, lambda s: 0),
            _mod_spec(layer, path, 3, D_MODEL, lambda s: 0),
            pl.BlockSpec((None, D_MODEL, FF_TC), lambda m, s: (layer, 0, ff_blk(s))),
            pl.BlockSpec((None, D_MODEL, FF_TC), lambda m, s: (layer, 0, N_FF_BLK + ff_blk(s))),
            pl.BlockSpec((None, D_FF, DOWN_TN), lambda m, s: (layer, 0, out_blk(s))),
            pl.BlockSpec((TM, DOWN_TN), lambda m, s: (m, out_blk(s))),
            _mod_spec(layer, path, 5, DOWN_TN, out_blk),
        ],
        out_specs=pl.BlockSpec((TM, DOWN_TN), lambda m, s: (m, out_blk(s))),
        out_shape=jax.ShapeDtypeStruct((N_TOK, D_MODEL), F32),
        scratch_shapes=[pltpu.VMEM((N_FF_BLK, TM, FF_TC), BF16),
                        pltpu.VMEM((TM, D_MODEL), BF16),
                        pltpu.VMEM((TM, 1), F32)],
        compiler_params=_params(2),
        name=f"ffn_{path}_l{layer}",
    )(x, norm_g.reshape(DEPTH, 1, D_MODEL), mod, mod, w_gate_up, w_gate_up, w_down, x, mod)


def kernel(x_prompt, x_sample, cache_k, cache_v, c, c_ctx, w_mod, b_mod, norm1_g, norm2_g,
           w_in, q_norm_g, k_norm_g, rpb, w_na_proj, w_fnet_proj, w_gate, w_o, w_gate_up, w_down):
    xs = {CTX: x_prompt.reshape(N_TOK, D_MODEL), LAT: x_sample.reshape(N_TOK, D_MODEL)}
    cond8 = jnp.concatenate([c_ctx[None, :], c, jnp.zeros((8 - 1 - DEC_BATCH, D_MODEL), F32)], axis=0)
    mod = _modulation(cond8, w_mod, b_mod).reshape(DEPTH, 8, 6, 1, D_MODEL)
    tbl = _bias_tables(rpb)

    reps = IN_TN // HEAD_DIM
    qkg = jnp.stack([jnp.tile(q_norm_g * (HEAD_DIM ** -0.5), (1, reps)),
                     jnp.tile(k_norm_g, (1, reps))], axis=1).reshape(DEPTH, 2, 1, IN_TN)
    head_id = np.arange(QK_SS_W) // HEAD_DIM
    ones_bd = jnp.asarray((head_id[:, None] == head_id[None, :]).astype(np.float32)).astype(BF16)
    cc, sc = _dft_tables(F_GROUP, F_GROUP ** -0.5)
    cs = jnp.asarray(np.concatenate([cc, sc], axis=1)).astype(BF16)

    ck = cache_k.reshape(DEC_BATCH, DEPTH, PAST_LEN, D_NA)
    cv = cache_v.reshape(DEC_BATCH, DEPTH, PAST_LEN, D_NA)

    kv = tuple(jnp.zeros((BATCH, DEPTH, SEQ, D_NA), F32) for _ in range(2))
    for l in range(DEPTH):
        for path in (CTX, LAT):
            x = xs[path]
            outs = _in_projection(l, path, x, mod, norm1_g, w_in, qkg, ones_bd, cs, kv_prev=kv)
            h, p, a_cos, a_sin = outs[:4]
            if path == CTX:
                kv = outs[4:]
                attn = _context_attention(l, p)
                f = _position_dft(l, SEQ, a_cos, a_sin)
            else:
                attn = _latent_attention(l, p, ck, cv, tbl)
                f = _position_dft(l, DEC_SEQ, a_cos, a_sin)
            x1 = _token_mixing(l, path, h, attn, f, x, mod, w_gate, w_na_proj, w_fnet_proj, w_o)
            xs[path] = _ffn(l, path, x1, mod, norm2_g, w_gate_up, w_down)

    new_k, new_v = (t.reshape(BATCH, DEPTH, SEQ, N_HEADS, HEAD_DIM) for t in kv)
    return (xs[CTX].reshape(BATCH, SEQ, D_MODEL), xs[LAT].reshape(DEC_BATCH, DEC_SEQ, D_MODEL),
            new_k, new_v)
```

```python
import numpy as np
import jax
import jax.numpy as jnp
from jax import lax
from jax.experimental import pallas as pl
from jax.experimental.pallas import tpu as pltpu

F32 = jnp.float32
BF16 = jnp.bfloat16

D_MODEL = 2048
BATCH = 16
SEQ = 256
DEPTH = 2
DEC_BATCH = 4
DEC_SEQ = 1024
PAST_LEN = 256
GRID_W = 64
ROWS = DEC_SEQ // GRID_W
N_HEADS = 16
HEAD_DIM = 64
D_NA = N_HEADS * HEAD_DIM
D_F = D_MODEL // 2
N_FGROUPS = 4
F_GROUP = D_F // N_FGROUPS
KH = 8
KW = 16
D_FF = 5632
D_IN = 3 * D_NA + D_F
EPS = 1e-6
N_DR = 2 * KH - 1
N_DC = 2 * KW - 1

N_TOK = BATCH * SEQ
assert N_TOK == DEC_BATCH * DEC_SEQ

V7X_LANES = 128
V7X_VMEM_LIMIT = 60 * 1024 * 1024

TM = 1024
N_MT = N_TOK // TM
NEG_INF = float("-inf")
CTX, LAT = "ctx", "lat"


def _params(n_axes):
    return pltpu.CompilerParams(dimension_semantics=("arbitrary",) * n_axes,
                                vmem_limit_bytes=V7X_VMEM_LIMIT)


def _mod_row(path, m, tile):
    return 0 if path == CTX else 1 + m // (DEC_SEQ // tile)


def _mod_spec(layer, path, chunk, width, col_map):
    return pl.BlockSpec((None, None, None, 1, width),
                        lambda m, s: (layer, _mod_row(path, m, TM), chunk, 0, col_map(s)))


def _tile_rows_spec(width, last_use):
    def index(m, s):
        return (jnp.minimum(m + jnp.where(s > last_use, 1, 0), N_MT - 1), 0)
    return pl.BlockSpec((TM, width), index)


def _sigmoid(z):
    return 1.0 / (1.0 + jnp.exp(-z))


def _bdot(a, b):
    return jnp.dot(a, b, preferred_element_type=F32)


MOD_TN = 1024


def _mod_kernel(cond_ref, w_ref, b_ref, o_ref):
    cnd = cond_ref[...]
    s = (cnd * _sigmoid(cnd)).astype(BF16)
    o_ref[...] = _bdot(s, w_ref[...].astype(BF16)) + b_ref[...]


def _modulation(cond8, w_mod, b_mod):
    n_blk = 6 * D_MODEL // MOD_TN
    return pl.pallas_call(
        _mod_kernel,
        grid=(DEPTH, n_blk),
        in_specs=[
            pl.BlockSpec((8, D_MODEL), lambda l, n: (0, 0)),
            pl.BlockSpec((None, D_MODEL, MOD_TN), lambda l, n: (l, 0, n)),
            pl.BlockSpec((None, 1, MOD_TN), lambda l, n: (l, 0, n)),
        ],
        out_specs=pl.BlockSpec((None, 8, MOD_TN), lambda l, n: (l, 0, n)),
        out_shape=jax.ShapeDtypeStruct((DEPTH, 8, 6 * D_MODEL), F32),
        compiler_params=_params(2),
        name="modulation",
    )(cond8, w_mod, b_mod.reshape(DEPTH, 1, 6 * D_MODEL))


def _bias_onehots():
    width = GRID_W * V7X_LANES
    oh_l = np.zeros((N_DC + 1, width), np.float32)
    oh_r = np.zeros((N_DC + 1, width), np.float32)
    mask = np.full((1, width), -np.inf, np.float32)
    for c in range(GRID_W):
        ws = min(max(c - KW // 2, 0), GRID_W - KW)
        for kc in range(ws, ws + KW):
            e = kc - c + KW - 1
            oh_l[e, c * V7X_LANES + kc] = 1.0
            oh_r[e, c * V7X_LANES + GRID_W + kc] = 1.0
            mask[0, c * V7X_LANES + kc] = 0.0
            mask[0, c * V7X_LANES + GRID_W + kc] = 0.0
    mask_last = mask.copy().reshape(GRID_W, V7X_LANES)
    mask_last[:, GRID_W:] = -np.inf
    return oh_l, oh_r, mask, mask_last.reshape(1, width)


def _split3(x):
    hi = x.astype(BF16)
    r1 = x - hi.astype(F32)
    mid = r1.astype(BF16)
    lo = (r1 - mid.astype(F32)).astype(BF16)
    return hi, mid, lo


def _bias_kernel(r1_ref, r2_ref, sel_ref, ohl_ref, ohr_ref, mask_ref, maskl_ref, o_ref):
    acc = None
    for r_ref, oh_ref in ((r1_ref, ohl_ref), (r2_ref, ohr_ref)):
        oh = oh_ref[...]
        for piece in _split3(r_ref[...]):
            t = _bdot(piece, oh)
            acc = t if acc is None else acc + t
    mask = jnp.where(sel_ref[...] > 0.5, maskl_ref[...], mask_ref[...])
    o_ref[...] = (acc + mask).reshape(o_ref.shape)


def _bias_tables(rpb):
    oh_l, oh_r, mask, mask_last = _bias_onehots()
    rows = N_HEADS * N_DR
    width = GRID_W * V7X_LANES
    pad = jnp.zeros((DEPTH, N_HEADS, N_DR, 1), F32)
    r1 = jnp.concatenate([rpb, pad], axis=-1).reshape(DEPTH * rows, N_DC + 1)
    nxt = jnp.concatenate([rpb[:, :, 1:], jnp.zeros((DEPTH, N_HEADS, 1, N_DC), F32)], axis=2)
    r2 = jnp.concatenate([nxt, pad], axis=-1).reshape(DEPTH * rows, N_DC + 1)
    sel = np.zeros((DEPTH, N_HEADS, N_DR, 1), np.float32)
    sel[:, :, N_DR - 1] = 1.0
    sel = jnp.asarray(sel.reshape(DEPTH * rows, 1))
    const = lambda shape: pl.BlockSpec(shape, lambda l: (0, 0))
    out = pl.pallas_call(
        _bias_kernel,
        grid=(DEPTH,),
        in_specs=[
            pl.BlockSpec((rows, N_DC + 1), lambda l: (l, 0)),
            pl.BlockSpec((rows, N_DC + 1), lambda l: (l, 0)),
            pl.BlockSpec((rows, 1), lambda l: (l, 0)),
            const((N_DC + 1, width)), const((N_DC + 1, width)),
            const((1, width)), const((1, width)),
        ],
        out_specs=pl.BlockSpec((rows, GRID_W, V7X_LANES), lambda l: (l, 0, 0)),
        out_shape=jax.ShapeDtypeStruct((DEPTH * rows, GRID_W, V7X_LANES), F32),
        compiler_params=_params(1),
        name="bias_tables",
    )(r1, r2, sel, jnp.asarray(oh_l).astype(BF16), jnp.asarray(oh_r).astype(BF16),
      jnp.asarray(mask), jnp.asarray(mask_last))
    return out.reshape(DEPTH, N_HEADS, N_DR, GRID_W, V7X_LANES)


NORM_RC = 128


def _norm_mod_rows(x_ref, g_ref, sc_ref, sh_ref, dst_ref, inv_ref, rows):
    def stats(i, carry):
        r0 = pl.multiple_of(i * NORM_RC, NORM_RC)
        x = x_ref[pl.ds(r0, NORM_RC), :]
        inv_ref[pl.ds(r0, NORM_RC), :] = lax.rsqrt(jnp.mean(x * x, axis=-1, keepdims=True) + EPS)
        return carry

    lax.fori_loop(0, rows // NORM_RC, stats, 0, unroll=True)
    gain = g_ref[...]
    scale1 = 1.0 + sc_ref[...]
    shift = sh_ref[...]

    def apply(i, carry):
        r0 = pl.multiple_of(i * NORM_RC, NORM_RC)
        y = x_ref[pl.ds(r0, NORM_RC), :] * inv_ref[pl.ds(r0, NORM_RC), :]
        dst_ref[pl.ds(r0, NORM_RC), :] = ((y * gain) * scale1 + shift).astype(BF16)
        return carry

    lax.fori_loop(0, rows // NORM_RC, apply, 0, unroll=True)


IN_TN = 512
N_IN_BLK = D_IN // IN_TN
IN_Q_BLKS = D_NA // IN_TN
IN_QK_BLKS = 2 * IN_Q_BLKS
IN_QKV_BLKS = 3 * IN_Q_BLKS
KV_TILE_B = TM // SEQ
QK_SS_W = 256


def _inproj_kernel(with_kv, x_ref, g_ref, sc_ref, sh_ref, w_ref, qkg_ref, ones_ref, cs_ref, *rest):
    if with_kv:
        h_ref, p_ref, ac_ref, as_ref, k_ref, v_ref, inv_s = rest[-7:]
    else:
        h_ref, p_ref, ac_ref, as_ref, inv_s = rest[-5:]
    n = pl.program_id(1)

    def project():
        return _bdot(h_ref[...], w_ref[...].astype(BF16))

    def finish_qk(y):
        y2 = (y * y).astype(BF16)
        ones = ones_ref[...]
        ss = jnp.concatenate([_bdot(y2[:, i * QK_SS_W:(i + 1) * QK_SS_W], ones)
                              for i in range(IN_TN // QK_SS_W)], axis=1)
        yn = y * lax.rsqrt(ss * (1.0 / HEAD_DIM) + EPS) * qkg_ref[...]
        p_ref[...] = yn.astype(BF16)
        if with_kv:
            @pl.when(n >= IN_Q_BLKS)
            def _():
                k_ref[...] = yn.reshape(KV_TILE_B, SEQ, IN_TN)

    def finish_v(y):
        p_ref[...] = y.astype(BF16)
        if with_kv:
            v_ref[...] = y.reshape(KV_TILE_B, SEQ, IN_TN)

    def finish_u(y):
        cs = cs_ref[...]
        for gg in range(IN_TN // F_GROUP):
            a = _bdot(y[:, gg * F_GROUP:(gg + 1) * F_GROUP].astype(BF16), cs)
            ac_ref[:, gg * F_GROUP:(gg + 1) * F_GROUP] = a[:, :F_GROUP].astype(BF16)
            as_ref[:, gg * F_GROUP:(gg + 1) * F_GROUP] = a[:, F_GROUP:].astype(BF16)

    @pl.when(n == 0)
    def _():
        _norm_mod_rows(x_ref, g_ref, sc_ref, sh_ref, h_ref, inv_s, TM)
        finish_qk(project())

    @pl.when(n > 0)
    def _():
        y = project()

        @pl.when(n < IN_QK_BLKS)
        def _():
            finish_qk(y)

        @pl.when(jnp.logical_and(n >= IN_QK_BLKS, n < IN_QKV_BLKS))
        def _():
            finish_v(y)

        @pl.when(n >= IN_QKV_BLKS)
        def _():
            finish_u(y)


def _in_projection(layer, path, x, mod, norm_g, w_in, qkg, ones_bd, cs, kv_prev=None):
    with_kv = path == CTX
    u_blk = lambda n: jnp.clip(n - IN_QKV_BLKS, 0, 1)
    in_specs = [
        _tile_rows_spec(D_MODEL, 0),
        pl.BlockSpec((None, 1, D_MODEL), lambda m, n: (layer, 0, 0)),
        _mod_spec(layer, path, 1, D_MODEL, lambda n: 0),
        _mod_spec(layer, path, 0, D_MODEL, lambda n: 0),
        pl.BlockSpec((None, D_MODEL, IN_TN), lambda m, n: (layer, 0, n)),
        pl.BlockSpec((None, None, 1, IN_TN),
                     lambda m, n: (layer, jnp.clip(n // IN_Q_BLKS, 0, 1), 0, 0)),
        pl.BlockSpec((QK_SS_W, QK_SS_W), lambda m, n: (0, 0)),
        pl.BlockSpec((F_GROUP, 2 * F_GROUP), lambda m, n: (0, 0)),
    ]
    args = [x, norm_g.reshape(DEPTH, 1, D_MODEL), mod, mod, w_in, qkg, ones_bd, cs]
    out_specs = [
        pl.BlockSpec((TM, D_MODEL), lambda m, n: (m, 0)),
        pl.BlockSpec((TM, IN_TN), lambda m, n: (m, jnp.minimum(n, IN_QKV_BLKS - 1))),
        pl.BlockSpec((TM, IN_TN), lambda m, n: (m, u_blk(n))),
        pl.BlockSpec((TM, IN_TN), lambda m, n: (m, u_blk(n))),
    ]
    out_shape = [
        jax.ShapeDtypeStruct((N_TOK, D_MODEL), BF16),
        jax.ShapeDtypeStruct((N_TOK, 3 * D_NA), BF16),
        jax.ShapeDtypeStruct((N_TOK, D_F), BF16),
        jax.ShapeDtypeStruct((N_TOK, D_F), BF16),
    ]
    aliases = {}
    if with_kv:
        kv_spec = lambda first: pl.BlockSpec(
            (KV_TILE_B, None, SEQ, IN_TN),
            lambda m, n: (m, layer, 0, jnp.clip(n - first, 0, IN_Q_BLKS - 1)))
        out_specs += [kv_spec(IN_Q_BLKS), kv_spec(IN_QK_BLKS)]
        out_shape += [jax.ShapeDtypeStruct((BATCH, DEPTH, SEQ, D_NA), F32)] * 2
        in_specs += [pl.BlockSpec(memory_space=pl.ANY)] * 2
        aliases = {len(args): 4, len(args) + 1: 5}
        args += list(kv_prev)

    def body(*refs):
        _inproj_kernel(with_kv, *refs)

    return pl.pallas_call(
        body,
        grid=(N_MT, N_IN_BLK),
        in_specs=in_specs,
        out_specs=out_specs,
        out_shape=out_shape,
        scratch_shapes=[pltpu.VMEM((TM, 1), F32)],
        input_output_aliases=aliases,
        compiler_params=_params(2),
        name=f"in_projection_{path}_l{layer}",
    )(*args)


def _stack_heads(q):
    lane = lax.broadcasted_iota(jnp.int32, q.shape, 1)
    zero = jnp.zeros_like(q)
    return jnp.concatenate([jnp.where(lane < HEAD_DIM, q, zero),
                            jnp.where(lane >= HEAD_DIM, q, zero)], axis=0)


def _unstack_heads(o2):
    rows = o2.shape[0] // 2
    lane = lax.broadcasted_iota(jnp.int32, (rows, V7X_LANES), 1)
    return jnp.where(lane < HEAD_DIM, o2[:rows], o2[rows:])


def _qk(q2, k):
    return lax.dot_general(q2, k, (((1,), (1,)), ((), ())), preferred_element_type=F32)


def _softmax(scores):
    mx = scores[0].max(axis=-1, keepdims=True)
    for s in scores[1:]:
        mx = jnp.maximum(mx, s.max(axis=-1, keepdims=True))
    exps = [jnp.exp(s - mx) for s in scores]
    den = exps[0].sum(axis=-1, keepdims=True)
    for e in exps[1:]:
        den = den + e.sum(axis=-1, keepdims=True)
    inv = 1.0 / den
    return [(e * inv).astype(BF16) for e in exps]


CTX_STEP_B = 4


def _ctx_attn_kernel(q_ref, k_ref, v_ref, o_ref):
    for b in range(CTX_STEP_B):
        rows = slice(b * SEQ, (b + 1) * SEQ)
        for p in range(N_HEADS // 2):
            cols = slice(p * V7X_LANES, (p + 1) * V7X_LANES)
            q2 = _stack_heads(q_ref[rows, cols])
            prob, = _softmax([_qk(q2, k_ref[rows, cols])])
            o2 = _bdot(prob, v_ref[rows, cols])
            o_ref[rows, cols] = _unstack_heads(o2).astype(BF16)


def _context_attention(layer, p):
    blk = CTX_STEP_B * SEQ
    return pl.pallas_call(
        _ctx_attn_kernel,
        grid=(BATCH // CTX_STEP_B,),
        in_specs=[pl.BlockSpec((blk, D_NA), lambda b: (b, 0)),
                  pl.BlockSpec((blk, D_NA), lambda b: (b, 1)),
                  pl.BlockSpec((blk, D_NA), lambda b: (b, 2))],
        out_specs=pl.BlockSpec((blk, D_NA), lambda b: (b, 0)),
        out_shape=jax.ShapeDtypeStruct((N_TOK, D_NA), BF16),
        compiler_params=_params(1),
        name=f"context_attention_l{layer}",
    )(p, p, p)


NA_QROWS = 4
NA_KROWS = 12
NA_KSTART = (0, 0, 4, 4)


def _na_bias_strip(tbl_ref, group, strip):
    hh, rr = divmod(strip, NA_QROWS)
    ks = NA_KSTART[group]
    lane = lax.broadcasted_iota(jnp.int32, (GRID_W, V7X_LANES), 1)
    ninf = jnp.full((GRID_W, V7X_LANES), NEG_INF, F32)
    r = NA_QROWS * group + rr
    rs = min(max(r - KH // 2, 0), ROWS - KH)
    tiles = []
    for a in range(NA_KROWS // 2):
        kk0 = ks + 2 * a
        ok0 = rs <= kk0 < rs + KH
        ok1 = rs <= kk0 + 1 < rs + KH
        dr0 = kk0 - r + KH - 1
        if ok0 and ok1:
            t = tbl_ref[hh, dr0]
        elif ok0:
            t = jnp.where(lane < GRID_W, tbl_ref[hh, dr0], ninf)
        elif ok1:
            t = jnp.where(lane >= GRID_W, tbl_ref[hh, dr0], ninf)
        else:
            t = ninf
        tiles.append(t)
    return jnp.concatenate(tiles, axis=1)


NA_PAIRS = 4
NA_LANES = NA_PAIRS * V7X_LANES


def _lat_attn_kernel(q_ref, k_ref, v_ref, kc_ref, vc_ref, tbl_ref, o_ref):
    qrows = NA_QROWS * GRID_W
    for pr in range(NA_PAIRS):
        cols = slice(pr * V7X_LANES, (pr + 1) * V7X_LANES)
        kc = kc_ref[:, cols].astype(BF16)
        vc = vc_ref[:, cols].astype(BF16)
        tbl = tbl_ref.at[2 * pr:2 * pr + 2]
        for g in range(ROWS // NA_QROWS):
            k0 = NA_KSTART[g] * GRID_W
            q2 = _stack_heads(q_ref[g * qrows:(g + 1) * qrows, cols])
            bias = jnp.concatenate([_na_bias_strip(tbl, g, t) for t in range(2 * NA_QROWS)], axis=0)
            s_loc = _qk(q2, k_ref[k0:k0 + NA_KROWS * GRID_W, cols]) + bias
            p_loc, p_ctx = _softmax([s_loc, _qk(q2, kc)])
            o2 = _bdot(p_loc, v_ref[k0:k0 + NA_KROWS * GRID_W, cols]) + _bdot(p_ctx, vc)
            o_ref[g * qrows:(g + 1) * qrows, cols] = _unstack_heads(o2).astype(BF16)


def _latent_attention(layer, p, cache_k, cache_v, tbl):
    n_blk = D_NA // NA_LANES
    cache_spec = pl.BlockSpec((None, None, PAST_LEN, NA_LANES), lambda b, h: (b, layer, 0, h))
    return pl.pallas_call(
        _lat_attn_kernel,
        grid=(DEC_BATCH, n_blk),
        in_specs=[
            pl.BlockSpec((DEC_SEQ, NA_LANES), lambda b, h: (b, h)),
            pl.BlockSpec((DEC_SEQ, NA_LANES), lambda b, h: (b, n_blk + h)),
            pl.BlockSpec((DEC_SEQ, NA_LANES), lambda b, h: (b, 2 * n_blk + h)),
            cache_spec, cache_spec,
            pl.BlockSpec((None, 2 * NA_PAIRS, N_DR, GRID_W, V7X_LANES), lambda b, h: (layer, h, 0, 0, 0)),
        ],
        out_specs=pl.BlockSpec((DEC_SEQ, NA_LANES), lambda b, h: (b, h)),
        out_shape=jax.ShapeDtypeStruct((N_TOK, D_NA), BF16),
        compiler_params=_params(2),
        name=f"latent_attention_l{layer}",
    )(p, p, p, cache_k, cache_v, tbl)


def _dft_tables(n, scale):
    j = np.arange(n, dtype=np.int64)
    ang = 2.0 * np.pi * ((j[:, None] * j[None, :]) % n).astype(np.float64) / n
    return (np.cos(ang) * scale).astype(np.float32), (np.sin(ang) * scale).astype(np.float32)


DFT_ROWS = 1024


def _pos_dft_kernel(seq, ct_ref, st_ref, ac_ref, as_ref, o_ref):
    ct = ct_ref[...]
    st = st_ref[...]
    for i in range(DFT_ROWS // seq):
        rows = slice(i * seq, (i + 1) * seq)
        o_ref[rows, :] = (_bdot(ct, ac_ref[rows, :]) - _bdot(st, as_ref[rows, :])).astype(BF16)


def _position_dft(layer, seq, a_cos, a_sin):
    ct, st = _dft_tables(seq, seq ** -0.5)
    blk = pl.BlockSpec((DFT_ROWS, D_F), lambda b: (b, 0))
    tab = pl.BlockSpec((seq, seq), lambda b: (0, 0))

    def body(*refs):
        _pos_dft_kernel(seq, *refs)

    return pl.pallas_call(
        body,
        grid=(N_TOK // DFT_ROWS,),
        in_specs=[tab, tab, blk, blk],
        out_specs=blk,
        out_shape=jax.ShapeDtypeStruct((N_TOK, D_F), BF16),
        compiler_params=_params(1),
        name=f"position_dft_{seq}_l{layer}",
    )(jnp.asarray(ct).astype(BF16), jnp.asarray(st).astype(BF16), a_cos, a_sin)


MIX_TC = 256
N_MIX_BLK = D_MODEL // MIX_TC
OUT_TN = 256
N_OUT_BLK = D_MODEL // OUT_TN


def _mix_kernel(h_ref, a_ref, f_ref, wgn_ref, wgf_ref, wna_ref, wf_ref, wo_ref, x_ref, g1_ref,
                o_ref, mix_s, wo_s):
    m = pl.program_id(0)
    s = pl.program_id(1)

    @pl.when(s < N_MIX_BLK)
    def _():
        h = h_ref[...]
        g_na = _sigmoid(_bdot(h, wgn_ref[...].astype(BF16)))
        g_fn = _sigmoid(_bdot(h, wgf_ref[...].astype(BF16)))
        na = _bdot(a_ref[...], wna_ref[...].astype(BF16))
        fn = _bdot(f_ref[...], wf_ref[...].astype(BF16))
        mix_s[s] = (g_na * na + g_fn * fn).astype(BF16)

    @pl.when(s >= N_MIX_BLK)
    def _():
        n = s - N_MIX_BLK

        @pl.when(m == 0)
        def _():
            wo_s[n] = wo_ref[...].astype(BF16)

        acc = None
        for c in range(N_MIX_BLK):
            t = _bdot(mix_s[c], wo_s[n, c * MIX_TC:(c + 1) * MIX_TC, :])
            acc = t if acc is None else acc + t
        o_ref[...] = x_ref[...] + g1_ref[...] * acc


def _token_mixing(layer, path, h, attn, f, x, mod, w_gate, w_na, w_f, w_o):
    mix_blk = lambda s: jnp.minimum(s, N_MIX_BLK - 1)
    out_blk = lambda s: jnp.clip(s - N_MIX_BLK, 0, N_OUT_BLK - 1)
    return pl.pallas_call(
        _mix_kernel,
        grid=(N_MT, N_MIX_BLK + N_OUT_BLK),
        in_specs=[
            _tile_rows_spec(D_MODEL, N_MIX_BLK - 1),
            _tile_rows_spec(D_NA, N_MIX_BLK - 1),
            _tile_rows_spec(D_F, N_MIX_BLK - 1),
            pl.BlockSpec((None, D_MODEL, MIX_TC), lambda m, s: (layer, 0, mix_blk(s))),
            pl.BlockSpec((None, D_MODEL, MIX_TC), lambda m, s: (layer, 0, N_MIX_BLK + mix_blk(s))),
            pl.BlockSpec((None, D_NA, MIX_TC), lambda m, s: (layer, 0, mix_blk(s))),
            pl.BlockSpec((None, D_F, MIX_TC), lambda m, s: (layer, 0, mix_blk(s))),
            pl.BlockSpec((None, D_MODEL, OUT_TN),
                         lambda m, s: (layer, 0, jnp.where(m == 0, out_blk(s), N_OUT_BLK - 1))),
            pl.BlockSpec((TM, OUT_TN), lambda m, s: (m, out_blk(s))),
            _mod_spec(layer, path, 2, OUT_TN, out_blk),
        ],
        out_specs=pl.BlockSpec((TM, OUT_TN), lambda m, s: (m, out_blk(s))),
        out_shape=jax.ShapeDtypeStruct((N_TOK, D_MODEL), F32),
        scratch_shapes=[pltpu.VMEM((N_MIX_BLK, TM, MIX_TC), BF16),
                        pltpu.VMEM((N_OUT_BLK, D_MODEL, OUT_TN), BF16)],
        compiler_params=_params(2),
        name=f"token_mixing_{path}_l{layer}",
    )(h, attn, f, w_gate, w_gate, w_na, w_f, w_o, x, mod)


FF_TC = 256
N_FF_BLK = D_FF // FF_TC
DOWN_TN = 256
N_DOWN_BLK = D_MODEL // DOWN_TN


def _ffn_kernel(xf_ref, ng_ref, sc_ref, sh_ref, wa_ref, wg_ref, wd_ref, x_ref, g2_ref, o_ref,
                act_s, h_s, inv_s):
    s = pl.program_id(1)

    def gated_chunk():
        h = h_s[...]
        g = _bdot(h, wg_ref[...].astype(BF16))
        gate = g * _sigmoid(g)
        a = _bdot(h, wa_ref[...].astype(BF16))
        act_s[s] = (gate * a).astype(BF16)

    @pl.when(s == 0)
    def _():
        _norm_mod_rows(xf_ref, ng_ref, sc_ref, sh_ref, h_s, inv_s, TM)
        gated_chunk()

    @pl.when(jnp.logical_and(s > 0, s < N_FF_BLK))
    def _():
        gated_chunk()

    @pl.when(s >= N_FF_BLK)
    def _():
        acc = None
        for j in range(N_FF_BLK):
            t = _bdot(act_s[j], wd_ref[j * FF_TC:(j + 1) * FF_TC, :].astype(BF16))
            acc = t if acc is None else acc + t
        o_ref[...] = x_ref[...] + g2_ref[...] * acc


def _ffn(layer, path, x, mod, norm_g, w_gate_up, w_down):
    ff_blk = lambda s: jnp.minimum(s, N_FF_BLK - 1)
    out_blk = lambda s: jnp.clip(s - N_FF_BLK, 0, N_DOWN_BLK - 1)
    return pl.pallas_call(
        _ffn_kernel,
        grid=(N_MT, N_FF_BLK + N_DOWN_BLK),
        in_specs=[
            _tile_rows_spec(D_MODEL, 0),
            pl.BlockSpec((None, 1, D_MODEL), lambda m, s: (layer, 0, 0)),
            _mod_spec(layer, path, 4, D_MODEL, lambda s: 0),
            _mod_spec(layer, path, 3, D_MODEL, lambda s: 0),
            pl.BlockSpec((None, D_MODEL, FF_TC), lambda m, s: (layer, 0, ff_blk(s))),
            pl.BlockSpec((None, D_MODEL, FF_TC), lambda m, s: (layer, 0, N_FF_BLK + ff_blk(s))),
            pl.BlockSpec((None, D_FF, DOWN_TN), lambda m, s: (layer, 0, out_blk(s))),
            pl.BlockSpec((TM, DOWN_TN), lambda m, s: (m, out_blk(s))),
            _mod_spec(layer, path, 5, DOWN_TN, out_blk),
        ],
        out_specs=pl.BlockSpec((TM, DOWN_TN), lambda m, s: (m, out_blk(s))),
        out_shape=jax.ShapeDtypeStruct((N_TOK, D_MODEL), F32),
        scratch_shapes=[pltpu.VMEM((N_FF_BLK, TM, FF_TC), BF16),
                        pltpu.VMEM((TM, D_MODEL), BF16),
                        pltpu.VMEM((TM, 1), F32)],
        compiler_params=_params(2),
        name=f"ffn_{path}_l{layer}",
    )(x, norm_g.reshape(DEPTH, 1, D_MODEL), mod, mod, w_gate_up, w_gate_up, w_down, x, mod)


def kernel(x_prompt, x_sample, cache_k, cache_v, c, c_ctx, w_mod, b_mod, norm1_g, norm2_g,
           w_in, q_norm_g, k_norm_g, rpb, w_na_proj, w_fnet_proj, w_gate, w_o, w_gate_up, w_down):
    xs = {CTX: x_prompt.reshape(N_TOK, D_MODEL), LAT: x_sample.reshape(N_TOK, D_MODEL)}
    cond8 = jnp.concatenate([c_ctx[None, :], c, jnp.zeros((8 - 1 - DEC_BATCH, D_MODEL), F32)], axis=0)
    mod = _modulation(cond8, w_mod, b_mod).reshape(DEPTH, 8, 6, 1, D_MODEL)
    tbl = _bias_tables(rpb)

    reps = IN_TN // HEAD_DIM
    qkg = jnp.stack([jnp.tile(q_norm_g * (HEAD_DIM ** -0.5), (1, reps)),
                     jnp.tile(k_norm_g, (1, reps))], axis=1).reshape(DEPTH, 2, 1, IN_TN)
    head_id = np.arange(QK_SS_W) // HEAD_DIM
    ones_bd = jnp.asarray((head_id[:, None] == head_id[None, :]).astype(np.float32)).astype(BF16)
    cc, sc = _dft_tables(F_GROUP, F_GROUP ** -0.5)
    cs = jnp.asarray(np.concatenate([cc, sc], axis=1)).astype(BF16)

    ck = cache_k.reshape(DEC_BATCH, DEPTH, PAST_LEN, D_NA)
    cv = cache_v.reshape(DEC_BATCH, DEPTH, PAST_LEN, D_NA)

    kv = tuple(jnp.zeros((BATCH, DEPTH, SEQ, D_NA), F32) for _ in range(2))
    for l in range(DEPTH):
        for path in (CTX, LAT):
            x = xs[path]
            outs = _in_projection(l, path, x, mod, norm1_g, w_in, qkg, ones_bd, cs, kv_prev=kv)
            h, p, a_cos, a_sin = outs[:4]
            if path == CTX:
                kv = outs[4:]
                attn = _context_attention(l, p)
                f = _position_dft(l, SEQ, a_cos, a_sin)
            else:
                attn = _latent_attention(l, p, ck, cv, tbl)
                f = _position_dft(l, DEC_SEQ, a_cos, a_sin)
            x1 = _token_mixing(l, path, h, attn, f, x, mod, w_gate, w_na_proj, w_fnet_proj, w_o)
            xs[path] = _ffn(l, path, x1, mod, norm2_g, w_gate_up, w_down)

    new_k, new_v = (t.reshape(BATCH, DEPTH, SEQ, N_HEADS, HEAD_DIM) for t in kv)
    return (xs[CTX].reshape(BATCH, SEQ, D_MODEL), xs[LAT].reshape(DEC_BATCH, DEC_SEQ, D_MODEL),
            new_k, new_v)
```

```python
import numpy as np
import jax
import jax.numpy as jnp
from jax import lax
from jax.experimental import pallas as pl
from jax.experimental.pallas import tpu as pltpu

F32 = jnp.float32
BF16 = jnp.bfloat16

D_MODEL = 2048
BATCH = 16
SEQ = 256
DEPTH = 2
DEC_BATCH = 4
DEC_SEQ = 1024
PAST_LEN = 256
GRID_W = 64
ROWS = DEC_SEQ // GRID_W
N_HEADS = 16
HEAD_DIM = 64
D_NA = N_HEADS * HEAD_DIM
D_F = D_MODEL // 2
N_FGROUPS = 4
F_GROUP = D_F // N_FGROUPS
KH = 8
KW = 16
D_FF = 5632
D_IN = 3 * D_NA + D_F
EPS = 1e-6
N_DR = 2 * KH - 1
N_DC = 2 * KW - 1

N_TOK = BATCH * SEQ
assert N_TOK == DEC_BATCH * DEC_SEQ

V7X_LANES = 128
V7X_VMEM_LIMIT = 60 * 1024 * 1024

TM = 1024
N_MT = N_TOK // TM
NEG_INF = float("-inf")
CTX, LAT = "ctx", "lat"


def _params(n_axes):
    return pltpu.CompilerParams(dimension_semantics=("arbitrary",) * n_axes,
                                vmem_limit_bytes=V7X_VMEM_LIMIT)


def _mod_row(path, m, tile):
    return 0 if path == CTX else 1 + m // (DEC_SEQ // tile)


def _mod_spec(layer, path, chunk, width, col_map):
    return pl.BlockSpec((None, None, None, 1, width),
                        lambda m, s: (layer, _mod_row(path, m, TM), chunk, 0, col_map(s)))


def _tile_rows_spec(width, last_use):
    def index(m, s):
        return (jnp.minimum(m + jnp.where(s > last_use, 1, 0), N_MT - 1), 0)
    return pl.BlockSpec((TM, width), index)


def _sigmoid(z):
    return 1.0 / (1.0 + jnp.exp(-z))


def _bdot(a, b):
    return jnp.dot(a, b, preferred_element_type=F32)


MOD_TN = 1024


def _mod_kernel(cond_ref, w_ref, b_ref, o_ref):
    cnd = cond_ref[...]
    s = (cnd * _sigmoid(cnd)).astype(BF16)
    o_ref[...] = _bdot(s, w_ref[...].astype(BF16)) + b_ref[...]


def _modulation(cond8, w_mod, b_mod):
    n_blk = 6 * D_MODEL // MOD_TN
    return pl.pallas_call(
        _mod_kernel,
        grid=(DEPTH, n_blk),
        in_specs=[
            pl.BlockSpec((8, D_MODEL), lambda l, n: (0, 0)),
            pl.BlockSpec((None, D_MODEL, MOD_TN), lambda l, n: (l, 0, n)),
            pl.BlockSpec((None, 1, MOD_TN), lambda l, n: (l, 0, n)),
        ],
        out_specs=pl.BlockSpec((None, 8, MOD_TN), lambda l, n: (l, 0, n)),
        out_shape=jax.ShapeDtypeStruct((DEPTH, 8, 6 * D_MODEL), F32),
        compiler_params=_params(2),
        name="modulation",
    )(cond8, w_mod, b_mod.reshape(DEPTH, 1, 6 * D_MODEL))


def _bias_onehots():
    width = GRID_W * V7X_LANES
    oh_l = np.zeros((N_DC + 1, width), np.float32)
    oh_r = np.zeros((N_DC + 1, width), np.float32)
    mask = np.full((1, width), -np.inf, np.float32)
    for c in range(GRID_W):
        ws = min(max(c - KW // 2, 0), GRID_W - KW)
        for kc in range(ws, ws + KW):
            e = kc - c + KW - 1
            oh_l[e, c * V7X_LANES + kc] = 1.0
            oh_r[e, c * V7X_LANES + GRID_W + kc] = 1.0
            mask[0, c * V7X_LANES + kc] = 0.0
            mask[0, c * V7X_LANES + GRID_W + kc] = 0.0
    mask_last = mask.copy().reshape(GRID_W, V7X_LANES)
    mask_last[:, GRID_W:] = -np.inf
    return oh_l, oh_r, mask, mask_last.reshape(1, width)


def _split3(x):
    hi = x.astype(BF16)
    r1 = x - hi.astype(F32)
    mid = r1.astype(BF16)
    lo = (r1 - mid.astype(F32)).astype(BF16)
    return hi, mid, lo


def _bias_kernel(r1_ref, r2_ref, sel_ref, ohl_ref, ohr_ref, mask_ref, maskl_ref, o_ref):
    acc = None
    for r_ref, oh_ref in ((r1_ref, ohl_ref), (r2_ref, ohr_ref)):
        oh = oh_ref[...]
        for piece in _split3(r_ref[...]):
            t = _bdot(piece, oh)
            acc = t if acc is None else acc + t
    mask = jnp.where(sel_ref[...] > 0.5, maskl_ref[...], mask_ref[...])
    o_ref[...] = (acc + mask).reshape(o_ref.shape)


def _bias_tables(rpb):
    oh_l, oh_r, mask, mask_last = _bias_onehots()
    rows = N_HEADS * N_DR
    width = GRID_W * V7X_LANES
    pad = jnp.zeros((DEPTH, N_HEADS, N_DR, 1), F32)
    r1 = jnp.concatenate([rpb, pad], axis=-1).reshape(DEPTH * rows, N_DC + 1)
    nxt = jnp.concatenate([rpb[:, :, 1:], jnp.zeros((DEPTH, N_HEADS, 1, N_DC), F32)], axis=2)
    r2 = jnp.concatenate([nxt, pad], axis=-1).reshape(DEPTH * rows, N_DC + 1)
    sel = np.zeros((DEPTH, N_HEADS, N_DR, 1), np.float32)
    sel[:, :, N_DR - 1] = 1.0
    sel = jnp.asarray(sel.reshape(DEPTH * rows, 1))
    const = lambda shape: pl.BlockSpec(shape, lambda l: (0, 0))
    out = pl.pallas_call(
        _bias_kernel,
        grid=(DEPTH,),
        in_specs=[
            pl.BlockSpec((rows, N_DC + 1), lambda l: (l, 0)),
            pl.BlockSpec((rows, N_DC + 1), lambda l: (l, 0)),
            pl.BlockSpec((rows, 1), lambda l: (l, 0)),
            const((N_DC + 1, width)), const((N_DC + 1, width)),
            const((1, width)), const((1, width)),
        ],
        out_specs=pl.BlockSpec((rows, GRID_W, V7X_LANES), lambda l: (l, 0, 0)),
        out_shape=jax.ShapeDtypeStruct((DEPTH * rows, GRID_W, V7X_LANES), F32),
        compiler_params=_params(1),
        name="bias_tables",
    )(r1, r2, sel, jnp.asarray(oh_l).astype(BF16), jnp.asarray(oh_r).astype(BF16),
      jnp.asarray(mask), jnp.asarray(mask_last))
    return out.reshape(DEPTH, N_HEADS, N_DR, GRID_W, V7X_LANES)


NORM_RC = 128


def _norm_mod_rows(x_ref, g_ref, sc_ref, sh_ref, dst_ref, inv_ref, rows):
    def stats(i, carry):
        r0 = pl.multiple_of(i * NORM_RC, NORM_RC)
        x = x_ref[pl.ds(r0, NORM_RC), :]
        inv_ref[pl.ds(r0, NORM_RC), :] = lax.rsqrt(jnp.mean(x * x, axis=-1, keepdims=True) + EPS)
        return carry

    lax.fori_loop(0, rows // NORM_RC, stats, 0, unroll=True)
    gain = g_ref[...]
    scale1 = 1.0 + sc_ref[...]
    shift = sh_ref[...]

    def apply(i, carry):
        r0 = pl.multiple_of(i * NORM_RC, NORM_RC)
        y = x_ref[pl.ds(r0, NORM_RC), :] * inv_ref[pl.ds(r0, NORM_RC), :]
        dst_ref[pl.ds(r0, NORM_RC), :] = ((y * gain) * scale1 + shift).astype(BF16)
        return carry

    lax.fori_loop(0, rows // NORM_RC, apply, 0, unroll=True)


IN_TN = 512
N_IN_BLK = D_IN // IN_TN
IN_Q_BLKS = D_NA // IN_TN
IN_QK_BLKS = 2 * IN_Q_BLKS
IN_QKV_BLKS = 3 * IN_Q_BLKS
KV_TILE_B = TM // SEQ
QK_SS_W = 256


def _inproj_kernel(with_kv, x_ref, g_ref, sc_ref, sh_ref, w_ref, qkg_ref, ones_ref, cs_ref, *rest):
    if with_kv:
        h_ref, p_ref, ac_ref, as_ref, k_ref, v_ref, inv_s = rest[-7:]
    else:
        h_ref, p_ref, ac_ref, as_ref, inv_s = rest[-5:]
    n = pl.program_id(1)

    def project():
        return _bdot(h_ref[...], w_ref[...].astype(BF16))

    def finish_qk(y):
        y2 = (y * y).astype(BF16)
        ones = ones_ref[...]
        ss = jnp.concatenate([_bdot(y2[:, i * QK_SS_W:(i + 1) * QK_SS_W], ones)
                              for i in range(IN_TN // QK_SS_W)], axis=1)
        yn = y * lax.rsqrt(ss * (1.0 / HEAD_DIM) + EPS) * qkg_ref[...]
        p_ref[...] = yn.astype(BF16)
        if with_kv:
            @pl.when(n >= IN_Q_BLKS)
            def _():
                k_ref[...] = yn.reshape(KV_TILE_B, SEQ, IN_TN)

    def finish_v(y):
        p_ref[...] = y.astype(BF16)
        if with_kv:
            v_ref[...] = y.reshape(KV_TILE_B, SEQ, IN_TN)

    def finish_u(y):
        cs = cs_ref[...]
        for gg in range(IN_TN // F_GROUP):
            a = _bdot(y[:, gg * F_GROUP:(gg + 1) * F_GROUP].astype(BF16), cs)
            ac_ref[:, gg * F_GROUP:(gg + 1) * F_GROUP] = a[:, :F_GROUP].astype(BF16)
            as_ref[:, gg * F_GROUP:(gg + 1) * F_GROUP] = a[:, F_GROUP:].astype(BF16)

    @pl.when(n == 0)
    def _():
        _norm_mod_rows(x_ref, g_ref, sc_ref, sh_ref, h_ref, inv_s, TM)
        finish_qk(project())

    @pl.when(n > 0)
    def _():
        y = project()

        @pl.when(n < IN_QK_BLKS)
        def _():
            finish_qk(y)

        @pl.when(jnp.logical_and(n >= IN_QK_BLKS, n < IN_QKV_BLKS))
        def _():
            finish_v(y)

        @pl.when(n >= IN_QKV_BLKS)
        def _():
            finish_u(y)


def _in_projection(layer, path, x, mod, norm_g, w_in, qkg, ones_bd, cs, kv_prev=None):
    with_kv = path == CTX
    u_blk = lambda n: jnp.clip(n - IN_QKV_BLKS, 0, 1)
    in_specs = [
        _tile_rows_spec(D_MODEL, 0),
        pl.BlockSpec((None, 1, D_MODEL), lambda m, n: (layer, 0, 0)),
        _mod_spec(layer, path, 1, D_MODEL, lambda n: 0),
        _mod_spec(layer, path, 0, D_MODEL, lambda n: 0),
        pl.BlockSpec((None, D_MODEL, IN_TN), lambda m, n: (layer, 0, n)),
        pl.BlockSpec((None, None, 1, IN_TN),
                     lambda m, n: (layer, jnp.clip(n // IN_Q_BLKS, 0, 1), 0, 0)),
        pl.BlockSpec((QK_SS_W, QK_SS_W), lambda m, n: (0, 0)),
        pl.BlockSpec((F_GROUP, 2 * F_GROUP), lambda m, n: (0, 0)),
    ]
    args = [x, norm_g.reshape(DEPTH, 1, D_MODEL), mod, mod, w_in, qkg, ones_bd, cs]
    out_specs = [
        pl.BlockSpec((TM, D_MODEL), lambda m, n: (m, 0)),
        pl.BlockSpec((TM, IN_TN), lambda m, n: (m, jnp.minimum(n, IN_QKV_BLKS - 1))),
        pl.BlockSpec((TM, IN_TN), lambda m, n: (m, u_blk(n))),
        pl.BlockSpec((TM, IN_TN), lambda m, n: (m, u_blk(n))),
    ]
    out_shape = [
        jax.ShapeDtypeStruct((N_TOK, D_MODEL), BF16),
        jax.ShapeDtypeStruct((N_TOK, 3 * D_NA), BF16),
        jax.ShapeDtypeStruct((N_TOK, D_F), BF16),
        jax.ShapeDtypeStruct((N_TOK, D_F), BF16),
    ]
    aliases = {}
    if with_kv:
        kv_spec = lambda first: pl.BlockSpec(
            (KV_TILE_B, None, SEQ, IN_TN),
            lambda m, n: (m, layer, 0, jnp.clip(n - first, 0, IN_Q_BLKS - 1)))
        out_specs += [kv_spec(IN_Q_BLKS), kv_spec(IN_QK_BLKS)]
        out_shape += [jax.ShapeDtypeStruct((BATCH, DEPTH, SEQ, D_NA), F32)] * 2
        in_specs += [pl.BlockSpec(memory_space=pl.ANY)] * 2
        aliases = {len(args): 4, len(args) + 1: 5}
        args += list(kv_prev)

    def body(*refs):
        _inproj_kernel(with_kv, *refs)

    return pl.pallas_call(
        body,
        grid=(N_MT, N_IN_BLK),
        in_specs=in_specs,
        out_specs=out_specs,
        out_shape=out_shape,
        scratch_shapes=[pltpu.VMEM((TM, 1), F32)],
        input_output_aliases=aliases,
        compiler_params=_params(2),
        name=f"in_projection_{path}_l{layer}",
    )(*args)


def _stack_heads(q):
    lane = lax.broadcasted_iota(jnp.int32, q.shape, 1)
    zero = jnp.zeros_like(q)
    return jnp.concatenate([jnp.where(lane < HEAD_DIM, q, zero),
                            jnp.where(lane >= HEAD_DIM, q, zero)], axis=0)


def _unstack_heads(o2):
    rows = o2.shape[0] // 2
    lane = lax.broadcasted_iota(jnp.int32, (rows, V7X_LANES), 1)
    return jnp.where(lane < HEAD_DIM, o2[:rows], o2[rows:])


def _qk(q2, k):
    return lax.dot_general(q2, k, (((1,), (1,)), ((), ())), preferred_element_type=F32)


def _softmax(scores):
    mx = scores[0].max(axis=-1, keepdims=True)
    for s in scores[1:]:
        mx = jnp.maximum(mx, s.max(axis=-1, keepdims=True))
    exps = [jnp.exp(s - mx) for s in scores]
    den = exps[0].sum(axis=-1, keepdims=True)
    for e in exps[1:]:
        den = den + e.sum(axis=-1, keepdims=True)
    inv = 1.0 / den
    return [(e * inv).astype(BF16) for e in exps]


CTX_STEP_B = 4


def _ctx_attn_kernel(q_ref, k_ref, v_ref, o_ref):
    for b in range(CTX_STEP_B):
        rows = slice(b * SEQ, (b + 1) * SEQ)
        for p in range(N_HEADS // 2):
            cols = slice(p * V7X_LANES, (p + 1) * V7X_LANES)
            q2 = _stack_heads(q_ref[rows, cols])
            prob, = _softmax([_qk(q2, k_ref[rows, cols])])
            o2 = _bdot(prob, v_ref[rows, cols])
            o_ref[rows, cols] = _unstack_heads(o2).astype(BF16)


def _context_attention(layer, p):
    blk = CTX_STEP_B * SEQ
    return pl.pallas_call(
        _ctx_attn_kernel,
        grid=(BATCH // CTX_STEP_B,),
        in_specs=[pl.BlockSpec((blk, D_NA), lambda b: (b, 0)),
                  pl.BlockSpec((blk, D_NA), lambda b: (b, 1)),
                  pl.BlockSpec((blk, D_NA), lambda b: (b, 2))],
        out_specs=pl.BlockSpec((blk, D_NA), lambda b: (b, 0)),
        out_shape=jax.ShapeDtypeStruct((N_TOK, D_NA), BF16),
        compiler_params=_params(1),
        name=f"context_attention_l{layer}",
    )(p, p, p)


NA_QROWS = 4
NA_KROWS = 12
NA_KSTART = (0, 0, 4, 4)


def _na_bias_strip(tbl_ref, group, strip):
    hh, rr = divmod(strip, NA_QROWS)
    ks = NA_KSTART[group]
    lane = lax.broadcasted_iota(jnp.int32, (GRID_W, V7X_LANES), 1)
    ninf = jnp.full((GRID_W, V7X_LANES), NEG_INF, F32)
    r = NA_QROWS * group + rr
    rs = min(max(r - KH // 2, 0), ROWS - KH)
    tiles = []
    for a in range(NA_KROWS // 2):
        kk0 = ks + 2 * a
        ok0 = rs <= kk0 < rs + KH
        ok1 = rs <= kk0 + 1 < rs + KH
        dr0 = kk0 - r + KH - 1
        if ok0 and ok1:
            t = tbl_ref[hh, dr0]
        elif ok0:
            t = jnp.where(lane < GRID_W, tbl_ref[hh, dr0], ninf)
        elif ok1:
            t = jnp.where(lane >= GRID_W, tbl_ref[hh, dr0], ninf)
        else:
            t = ninf
        tiles.append(t)
    return jnp.concatenate(tiles, axis=1)


NA_PAIRS = 4
NA_LANES = NA_PAIRS * V7X_LANES


def _lat_attn_kernel(q_ref, k_ref, v_ref, kc_ref, vc_ref, tbl_ref, o_ref):
    qrows = NA_QROWS * GRID_W
    for pr in range(NA_PAIRS):
        cols = slice(pr * V7X_LANES, (pr + 1) * V7X_LANES)
        kc = kc_ref[:, cols].astype(BF16)
        vc = vc_ref[:, cols].astype(BF16)
        tbl = tbl_ref.at[2 * pr:2 * pr + 2]
        for g in range(ROWS // NA_QROWS):
            k0 = NA_KSTART[g] * GRID_W
            q2 = _stack_heads(q_ref[g * qrows:(g + 1) * qrows, cols])
            bias = jnp.concatenate([_na_bias_strip(tbl, g, t) for t in range(2 * NA_QROWS)], axis=0)
            s_loc = _qk(q2, k_ref[k0:k0 + NA_KROWS * GRID_W, cols]) + bias
            p_loc, p_ctx = _softmax([s_loc, _qk(q2, kc)])
            o2 = _bdot(p_loc, v_ref[k0:k0 + NA_KROWS * GRID_W, cols]) + _bdot(p_ctx, vc)
            o_ref[g * qrows:(g + 1) * qrows, cols] = _unstack_heads(o2).astype(BF16)


def _latent_attention(layer, p, cache_k, cache_v, tbl):
    n_blk = D_NA // NA_LANES
    cache_spec = pl.BlockSpec((None, None, PAST_LEN, NA_LANES), lambda b, h: (b, layer, 0, h))
    return pl.pallas_call(
        _lat_attn_kernel,
        grid=(DEC_BATCH, n_blk),
        in_specs=[
            pl.BlockSpec((DEC_SEQ, NA_LANES), lambda b, h: (b, h)),
            pl.BlockSpec((DEC_SEQ, NA_LANES), lambda b, h: (b, n_blk + h)),
            pl.BlockSpec((DEC_SEQ, NA_LANES), lambda b, h: (b, 2 * n_blk + h)),
            cache_spec, cache_spec,
            pl.BlockSpec((None, 2 * NA_PAIRS, N_DR, GRID_W, V7X_LANES), lambda b, h: (layer, h, 0, 0, 0)),
        ],
        out_specs=pl.BlockSpec((DEC_SEQ, NA_LANES), lambda b, h: (b, h)),
        out_shape=jax.ShapeDtypeStruct((N_TOK, D_NA), BF16),
        compiler_params=_params(2),
        name=f"latent_attention_l{layer}",
    )(p, p, p, cache_k, cache_v, tbl)


def _dft_tables(n, scale):
    j = np.arange(n, dtype=np.int64)
    ang = 2.0 * np.pi * ((j[:, None] * j[None, :]) % n).astype(np.float64) / n
    return (np.cos(ang) * scale).astype(np.float32), (np.sin(ang) * scale).astype(np.float32)


DFT_ROWS = 1024


def _pos_dft_kernel(seq, ct_ref, st_ref, ac_ref, as_ref, o_ref):
    ct = ct_ref[...]
    st = st_ref[...]
    for i in range(DFT_ROWS // seq):
        rows = slice(i * seq, (i + 1) * seq)
        o_ref[rows, :] = (_bdot(ct, ac_ref[rows, :]) - _bdot(st, as_ref[rows, :])).astype(BF16)


def _position_dft(layer, seq, a_cos, a_sin):
    ct, st = _dft_tables(seq, seq ** -0.5)
    blk = pl.BlockSpec((DFT_ROWS, D_F), lambda b: (b, 0))
    tab = pl.BlockSpec((seq, seq), lambda b: (0, 0))

    def body(*refs):
        _pos_dft_kernel(seq, *refs)

    return pl.pallas_call(
        body,
        grid=(N_TOK // DFT_ROWS,),
        in_specs=[tab, tab, blk, blk],
        out_specs=blk,
        out_shape=jax.ShapeDtypeStruct((N_TOK, D_F), BF16),
        compiler_params=_params(1),
        name=f"position_dft_{seq}_l{layer}",
    )(jnp.asarray(ct).astype(BF16), jnp.asarray(st).astype(BF16), a_cos, a_sin)


MIX_TC = 256
N_MIX_BLK = D_MODEL // MIX_TC
OUT_TN = 256
N_OUT_BLK = D_MODEL // OUT_TN


def _mix_kernel(h_ref, a_ref, f_ref, wgn_ref, wgf_ref, wna_ref, wf_ref, wo_ref, x_ref, g1_ref,
                o_ref, mix_s, wo_s, wna_s, wf_s):
    m = pl.program_id(0)
    s = pl.program_id(1)

    @pl.when(s < N_MIX_BLK)
    def _():
        @pl.when(m == 0)
        def _():
            wna_s[s] = wna_ref[...].astype(BF16)
            wf_s[s] = wf_ref[...].astype(BF16)

        h = h_ref[...]
        g_na = _sigmoid(_bdot(h, wgn_ref[...].astype(BF16)))
        g_fn = _sigmoid(_bdot(h, wgf_ref[...].astype(BF16)))
        na = _bdot(a_ref[...], wna_s[s])
        fn = _bdot(f_ref[...], wf_s[s])
        mix_s[s] = (g_na * na + g_fn * fn).astype(BF16)

    @pl.when(s >= N_MIX_BLK)
    def _():
        n = s - N_MIX_BLK

        @pl.when(m == 0)
        def _():
            wo_s[n] = wo_ref[...].astype(BF16)

        acc = None
        for c in range(N_MIX_BLK):
            t = _bdot(mix_s[c], wo_s[n, c * MIX_TC:(c + 1) * MIX_TC, :])
            acc = t if acc is None else acc + t
        o_ref[...] = x_ref[...] + g1_ref[...] * acc


def _token_mixing(layer, path, h, attn, f, x, mod, w_gate, w_na, w_f, w_o):
    mix_blk = lambda s: jnp.minimum(s, N_MIX_BLK - 1)
    out_blk = lambda s: jnp.clip(s - N_MIX_BLK, 0, N_OUT_BLK - 1)
    first_tile_blk = lambda m, s: jnp.where(m == 0, mix_blk(s), N_MIX_BLK - 1)
    return pl.pallas_call(
        _mix_kernel,
        grid=(N_MT, N_MIX_BLK + N_OUT_BLK),
        in_specs=[
            _tile_rows_spec(D_MODEL, N_MIX_BLK - 1),
            _tile_rows_spec(D_NA, N_MIX_BLK - 1),
            _tile_rows_spec(D_F, N_MIX_BLK - 1),
            pl.BlockSpec((None, D_MODEL, MIX_TC), lambda m, s: (layer, 0, mix_blk(s))),
            pl.BlockSpec((None, D_MODEL, MIX_TC), lambda m, s: (layer, 0, N_MIX_BLK + mix_blk(s))),
            pl.BlockSpec((None, D_NA, MIX_TC), lambda m, s: (layer, 0, first_tile_blk(m, s))),
            pl.BlockSpec((None, D_F, MIX_TC), lambda m, s: (layer, 0, first_tile_blk(m, s))),
            pl.BlockSpec((None, D_MODEL, OUT_TN),
                         lambda m, s: (layer, 0, jnp.where(m == 0, out_blk(s), N_OUT_BLK - 1))),
            pl.BlockSpec((TM, OUT_TN), lambda m, s: (m, out_blk(s))),
            _mod_spec(layer, path, 2, OUT_TN, out_blk),
        ],
        out_specs=pl.BlockSpec((TM, OUT_TN), lambda m, s: (m, out_blk(s))),
        out_shape=jax.ShapeDtypeStruct((N_TOK, D_MODEL), F32),
        scratch_shapes=[pltpu.VMEM((N_MIX_BLK, TM, MIX_TC), BF16),
                        pltpu.VMEM((N_OUT_BLK, D_MODEL, OUT_TN), BF16),
                        pltpu.VMEM((N_MIX_BLK, D_NA, MIX_TC), BF16),
                        pltpu.VMEM((N_MIX_BLK, D_F, MIX_TC), BF16)],
        compiler_params=_params(2),
        name=f"token_mixing_{path}_l{layer}",
    )(h, attn, f, w_gate, w_gate, w_na, w_f, w_o, x, mod)


FF_TC = 256
N_FF_BLK = D_FF // FF_TC
DOWN_TN = 256
N_DOWN_BLK = D_MODEL // DOWN_TN


def _ffn_kernel(xf_ref, ng_ref, sc_ref, sh_ref, wa_ref, wg_ref, wd_ref, x_ref, g2_ref, o_ref,
                act_s, h_s, inv_s):
    s = pl.program_id(1)

    def gated_chunk():
        h = h_s[...]
        g = _bdot(h, wg_ref[...].astype(BF16))
        gate = g * _sigmoid(g)
        a = _bdot(h, wa_ref[...].astype(BF16))
        act_s[s] = (gate * a).astype(BF16)

    @pl.when(s == 0)
    def _():
        _norm_mod_rows(xf_ref, ng_ref, sc_ref, sh_ref, h_s, inv_s, TM)
        gated_chunk()

    @pl.when(jnp.logical_and(s > 0, s < N_FF_BLK))
    def _():
        gated_chunk()

    @pl.when(s >= N_FF_BLK)
    def _():
        acc = None
        for j in range(N_FF_BLK):
            t = _bdot(act_s[j], wd_ref[j * FF_TC:(j + 1) * FF_TC, :].astype(BF16))
            acc = t if acc is None else acc + t
        o_ref[...] = x_ref[...] + g2_ref[...] * acc


def _ffn(layer, path, x, mod, norm_g, w_gate_up, w_down):
    ff_blk = lambda s: jnp.minimum(s, N_FF_BLK - 1)
    out_blk = lambda s: jnp.clip(s - N_FF_BLK, 0, N_DOWN_BLK - 1)
    return pl.pallas_call(
        _ffn_kernel,
        grid=(N_MT, N_FF_BLK + N_DOWN_BLK),
        in_specs=[
            _tile_rows_spec(D_MODEL, 0),
            pl.BlockSpec((None, 1, D_MODEL), lambda m, s: (layer, 0, 0)),
            _mod_spec(layer, path, 4, D_MODEL, lambda s: 0),
            _mod_spec(layer, path, 3, D_MODEL, lambda s: 0),
            pl.BlockSpec((None, D_MODEL, FF_TC), lambda m, s: (layer, 0, ff_blk(s))),
            pl.BlockSpec((None, D_MODEL, FF_TC), lambda m, s: (layer, 0, N_FF_BLK + ff_blk(s))),
            pl.BlockSpec((None, D_FF, DOWN_TN), lambda m, s: (layer, 0, out_blk(s))),
            pl.BlockSpec((TM, DOWN_TN), lambda m, s: (m, out_blk(s))),
            _mod_spec(layer, path, 5, DOWN_TN, out_blk),
        ],
        out_specs=pl.BlockSpec((TM, DOWN_TN), lambda m, s: (m, out_blk(s))),
        out_shape=jax.ShapeDtypeStruct((N_TOK, D_MODEL), F32),
        scratch_shapes=[pltpu.VMEM((N_FF_BLK, TM, FF_TC), BF16),
                        pltpu.VMEM((TM, D_MODEL), BF16),
                        pltpu.VMEM((TM, 1), F32)],
        compiler_params=_params(2),
        name=f"ffn_{path}_l{layer}",
    )(x, norm_g.reshape(DEPTH, 1, D_MODEL), mod, mod, w_gate_up, w_gate_up, w_down, x, mod)


def kernel(x_prompt, x_sample, cache_k, cache_v, c, c_ctx, w_mod, b_mod, norm1_g, norm2_g,
           w_in, q_norm_g, k_norm_g, rpb, w_na_proj, w_fnet_proj, w_gate, w_o, w_gate_up, w_down):
    xs = {CTX: x_prompt.reshape(N_TOK, D_MODEL), LAT: x_sample.reshape(N_TOK, D_MODEL)}
    cond8 = jnp.concatenate([c_ctx[None, :], c, jnp.zeros((8 - 1 - DEC_BATCH, D_MODEL), F32)], axis=0)
    mod = _modulation(cond8, w_mod, b_mod).reshape(DEPTH, 8, 6, 1, D_MODEL)
    tbl = _bias_tables(rpb)

    reps = IN_TN // HEAD_DIM
    qkg = jnp.stack([jnp.tile(q_norm_g * (HEAD_DIM ** -0.5), (1, reps)),
                     jnp.tile(k_norm_g, (1, reps))], axis=1).reshape(DEPTH, 2, 1, IN_TN)
    head_id = np.arange(QK_SS_W) // HEAD_DIM
    ones_bd = jnp.asarray((head_id[:, None] == head_id[None, :]).astype(np.float32)).astype(BF16)
    cc, sc = _dft_tables(F_GROUP, F_GROUP ** -0.5)
    cs = jnp.asarray(np.concatenate([cc, sc], axis=1)).astype(BF16)

    ck = cache_k.reshape(DEC_BATCH, DEPTH, PAST_LEN, D_NA)
    cv = cache_v.reshape(DEC_BATCH, DEPTH, PAST_LEN, D_NA)

    kv = tuple(jnp.zeros((BATCH, DEPTH, SEQ, D_NA), F32) for _ in range(2))
    for l in range(DEPTH):
        for path in (CTX, LAT):
            x = xs[path]
            outs = _in_projection(l, path, x, mod, norm1_g, w_in, qkg, ones_bd, cs, kv_prev=kv)
            h, p, a_cos, a_sin = outs[:4]
            if path == CTX:
                kv = outs[4:]
                attn = _context_attention(l, p)
                f = _position_dft(l, SEQ, a_cos, a_sin)
            else:
                attn = _latent_attention(l, p, ck, cv, tbl)
                f = _position_dft(l, DEC_SEQ, a_cos, a_sin)
            x1 = _token_mixing(l, path, h, attn, f, x, mod, w_gate, w_na_proj, w_fnet_proj, w_o)
            xs[path] = _ffn(l, path, x1, mod, norm2_g, w_gate_up, w_down)

    new_k, new_v = (t.reshape(BATCH, DEPTH, SEQ, N_HEADS, HEAD_DIM) for t in kv)
    return (xs[CTX].reshape(BATCH, SEQ, D_MODEL), xs[LAT].reshape(DEC_BATCH, DEC_SEQ, D_MODEL),
            new_k, new_v)
```

```python
import numpy as np
import jax
import jax.numpy as jnp
from jax import lax
from jax.experimental import pallas as pl
from jax.experimental.pallas import tpu as pltpu

F32 = jnp.float32
BF16 = jnp.bfloat16

D_MODEL = 2048
BATCH = 16
SEQ = 256
DEPTH = 2
DEC_BATCH = 4
DEC_SEQ = 1024
PAST_LEN = 256
GRID_W = 64
ROWS = DEC_SEQ // GRID_W
N_HEADS = 16
HEAD_DIM = 64
D_NA = N_HEADS * HEAD_DIM
D_F = D_MODEL // 2
N_FGROUPS = 4
F_GROUP = D_F // N_FGROUPS
KH = 8
KW = 16
D_FF = 5632
D_IN = 3 * D_NA + D_F
EPS = 1e-6
N_DR = 2 * KH - 1
N_DC = 2 * KW - 1

N_TOK = BATCH * SEQ
assert N_TOK == DEC_BATCH * DEC_SEQ

V7X_LANES = 128
V7X_VMEM_LIMIT = 60 * 1024 * 1024

TM = 1024
N_MT = N_TOK // TM
NEG_INF = float("-inf")
CTX, LAT = "ctx", "lat"


def _params(n_axes):
    return pltpu.CompilerParams(dimension_semantics=("arbitrary",) * n_axes,
                                vmem_limit_bytes=V7X_VMEM_LIMIT)


def _mod_row(path, m, tile):
    return 0 if path == CTX else 1 + m // (DEC_SEQ // tile)


def _mod_spec(layer, path, chunk, width, col_map):
    return pl.BlockSpec((None, None, None, 1, width),
                        lambda m, s: (layer, _mod_row(path, m, TM), chunk, 0, col_map(s)))


def _tile_rows_spec(width, last_use):
    def index(m, s):
        return (jnp.minimum(m + jnp.where(s > last_use, 1, 0), N_MT - 1), 0)
    return pl.BlockSpec((TM, width), index)


def _sigmoid(z):
    return 1.0 / (1.0 + jnp.exp(-z))


def _bdot(a, b):
    return jnp.dot(a, b, preferred_element_type=F32)


MOD_TN = 1024


def _mod_kernel(cond_ref, w_ref, b_ref, o_ref):
    cnd = cond_ref[...]
    s = (cnd * _sigmoid(cnd)).astype(BF16)
    o_ref[...] = _bdot(s, w_ref[...].astype(BF16)) + b_ref[...]


def _modulation(cond8, w_mod, b_mod):
    n_blk = 6 * D_MODEL // MOD_TN
    return pl.pallas_call(
        _mod_kernel,
        grid=(DEPTH, n_blk),
        in_specs=[
            pl.BlockSpec((8, D_MODEL), lambda l, n: (0, 0)),
            pl.BlockSpec((None, D_MODEL, MOD_TN), lambda l, n: (l, 0, n)),
            pl.BlockSpec((None, 1, MOD_TN), lambda l, n: (l, 0, n)),
        ],
        out_specs=pl.BlockSpec((None, 8, MOD_TN), lambda l, n: (l, 0, n)),
        out_shape=jax.ShapeDtypeStruct((DEPTH, 8, 6 * D_MODEL), F32),
        compiler_params=_params(2),
        name="modulation",
    )(cond8, w_mod, b_mod.reshape(DEPTH, 1, 6 * D_MODEL))


def _bias_onehots():
    width = GRID_W * V7X_LANES
    oh_l = np.zeros((N_DC + 1, width), np.float32)
    oh_r = np.zeros((N_DC + 1, width), np.float32)
    mask = np.full((1, width), -np.inf, np.float32)
    for c in range(GRID_W):
        ws = min(max(c - KW // 2, 0), GRID_W - KW)
        for kc in range(ws, ws + KW):
            e = kc - c + KW - 1
            oh_l[e, c * V7X_LANES + kc] = 1.0
            oh_r[e, c * V7X_LANES + GRID_W + kc] = 1.0
            mask[0, c * V7X_LANES + kc] = 0.0
            mask[0, c * V7X_LANES + GRID_W + kc] = 0.0
    mask_last = mask.copy().reshape(GRID_W, V7X_LANES)
    mask_last[:, GRID_W:] = -np.inf
    return oh_l, oh_r, mask, mask_last.reshape(1, width)


def _split3(x):
    hi = x.astype(BF16)
    r1 = x - hi.astype(F32)
    mid = r1.astype(BF16)
    lo = (r1 - mid.astype(F32)).astype(BF16)
    return hi, mid, lo


def _bias_kernel(r1_ref, r2_ref, sel_ref, ohl_ref, ohr_ref, mask_ref, maskl_ref, o_ref):
    acc = None
    for r_ref, oh_ref in ((r1_ref, ohl_ref), (r2_ref, ohr_ref)):
        oh = oh_ref[...]
        for piece in _split3(r_ref[...]):
            t = _bdot(piece, oh)
            acc = t if acc is None else acc + t
    mask = jnp.where(sel_ref[...] > 0.5, maskl_ref[...], mask_ref[...])
    o_ref[...] = (acc + mask).reshape(o_ref.shape)


def _bias_tables(rpb):
    oh_l, oh_r, mask, mask_last = _bias_onehots()
    rows = N_HEADS * N_DR
    width = GRID_W * V7X_LANES
    pad = jnp.zeros((DEPTH, N_HEADS, N_DR, 1), F32)
    r1 = jnp.concatenate([rpb, pad], axis=-1).reshape(DEPTH * rows, N_DC + 1)
    nxt = jnp.concatenate([rpb[:, :, 1:], jnp.zeros((DEPTH, N_HEADS, 1, N_DC), F32)], axis=2)
    r2 = jnp.concatenate([nxt, pad], axis=-1).reshape(DEPTH * rows, N_DC + 1)
    sel = np.zeros((DEPTH, N_HEADS, N_DR, 1), np.float32)
    sel[:, :, N_DR - 1] = 1.0
    sel = jnp.asarray(sel.reshape(DEPTH * rows, 1))
    const = lambda shape: pl.BlockSpec(shape, lambda l: (0, 0))
    out = pl.pallas_call(
        _bias_kernel,
        grid=(DEPTH,),
        in_specs=[
            pl.BlockSpec((rows, N_DC + 1), lambda l: (l, 0)),
            pl.BlockSpec((rows, N_DC + 1), lambda l: (l, 0)),
            pl.BlockSpec((rows, 1), lambda l: (l, 0)),
            const((N_DC + 1, width)), const((N_DC + 1, width)),
            const((1, width)), const((1, width)),
        ],
        out_specs=pl.BlockSpec((rows, GRID_W, V7X_LANES), lambda l: (l, 0, 0)),
        out_shape=jax.ShapeDtypeStruct((DEPTH * rows, GRID_W, V7X_LANES), F32),
        compiler_params=_params(1),
        name="bias_tables",
    )(r1, r2, sel, jnp.asarray(oh_l).astype(BF16), jnp.asarray(oh_r).astype(BF16),
      jnp.asarray(mask), jnp.asarray(mask_last))
    return out.reshape(DEPTH, N_HEADS, N_DR, GRID_W, V7X_LANES)


NORM_RC = 128


def _norm_mod_rows(x_ref, g_ref, sc_ref, sh_ref, dst_ref, inv_ref, rows):
    def stats(i, carry):
        r0 = pl.multiple_of(i * NORM_RC, NORM_RC)
        x = x_ref[pl.ds(r0, NORM_RC), :]
        inv_ref[pl.ds(r0, NORM_RC), :] = lax.rsqrt(jnp.mean(x * x, axis=-1, keepdims=True) + EPS)
        return carry

    lax.fori_loop(0, rows // NORM_RC, stats, 0, unroll=True)
    gain = g_ref[...]
    scale1 = 1.0 + sc_ref[...]
    shift = sh_ref[...]

    def apply(i, carry):
        r0 = pl.multiple_of(i * NORM_RC, NORM_RC)
        y = x_ref[pl.ds(r0, NORM_RC), :] * inv_ref[pl.ds(r0, NORM_RC), :]
        dst_ref[pl.ds(r0, NORM_RC), :] = ((y * gain) * scale1 + shift).astype(BF16)
        return carry

    lax.fori_loop(0, rows // NORM_RC, apply, 0, unroll=True)


IN_TN = 512
N_IN_BLK = D_IN // IN_TN
IN_Q_BLKS = D_NA // IN_TN
IN_QK_BLKS = 2 * IN_Q_BLKS
IN_QKV_BLKS = 3 * IN_Q_BLKS
KV_TILE_B = TM // SEQ
QK_SS_W = 256


def _inproj_kernel(with_kv, x_ref, g_ref, sc_ref, sh_ref, w_ref, qkg_ref, ones_ref, cs_ref, *rest):
    if with_kv:
        h_ref, p_ref, ac_ref, as_ref, k_ref, v_ref, inv_s = rest[-7:]
    else:
        h_ref, p_ref, ac_ref, as_ref, inv_s = rest[-5:]
    n = pl.program_id(1)

    def project():
        return _bdot(h_ref[...], w_ref[...].astype(BF16))

    def finish_qk(y):
        y2 = (y * y).astype(BF16)
        ones = ones_ref[...]
        ss = jnp.concatenate([_bdot(y2[:, i * QK_SS_W:(i + 1) * QK_SS_W], ones)
                              for i in range(IN_TN // QK_SS_W)], axis=1)
        yn = y * lax.rsqrt(ss * (1.0 / HEAD_DIM) + EPS) * qkg_ref[...]
        p_ref[...] = yn.astype(BF16)
        if with_kv:
            @pl.when(n >= IN_Q_BLKS)
            def _():
                k_ref[...] = yn.reshape(KV_TILE_B, SEQ, IN_TN)

    def finish_v(y):
        p_ref[...] = y.astype(BF16)
        if with_kv:
            v_ref[...] = y.reshape(KV_TILE_B, SEQ, IN_TN)

    def finish_u(y):
        cs = cs_ref[...]
        for gg in range(IN_TN // F_GROUP):
            a = _bdot(y[:, gg * F_GROUP:(gg + 1) * F_GROUP].astype(BF16), cs)
            ac_ref[:, gg * F_GROUP:(gg + 1) * F_GROUP] = a[:, :F_GROUP].astype(BF16)
            as_ref[:, gg * F_GROUP:(gg + 1) * F_GROUP] = a[:, F_GROUP:].astype(BF16)

    @pl.when(n == 0)
    def _():
        _norm_mod_rows(x_ref, g_ref, sc_ref, sh_ref, h_ref, inv_s, TM)
        finish_qk(project())

    @pl.when(n > 0)
    def _():
        y = project()

        @pl.when(n < IN_QK_BLKS)
        def _():
            finish_qk(y)

        @pl.when(jnp.logical_and(n >= IN_QK_BLKS, n < IN_QKV_BLKS))
        def _():
            finish_v(y)

        @pl.when(n >= IN_QKV_BLKS)
        def _():
            finish_u(y)


def _in_projection(layer, path, x, mod, norm_g, w_in, qkg, ones_bd, cs, kv_prev=None):
    with_kv = path == CTX
    u_blk = lambda n: jnp.clip(n - IN_QKV_BLKS, 0, 1)
    in_specs = [
        _tile_rows_spec(D_MODEL, 0),
        pl.BlockSpec((None, 1, D_MODEL), lambda m, n: (layer, 0, 0)),
        _mod_spec(layer, path, 1, D_MODEL, lambda n: 0),
        _mod_spec(layer, path, 0, D_MODEL, lambda n: 0),
        pl.BlockSpec((None, D_MODEL, IN_TN), lambda m, n: (layer, 0, n)),
        pl.BlockSpec((None, None, 1, IN_TN),
                     lambda m, n: (layer, jnp.clip(n // IN_Q_BLKS, 0, 1), 0, 0)),
        pl.BlockSpec((QK_SS_W, QK_SS_W), lambda m, n: (0, 0)),
        pl.BlockSpec((F_GROUP, 2 * F_GROUP), lambda m, n: (0, 0)),
    ]
    args = [x, norm_g.reshape(DEPTH, 1, D_MODEL), mod, mod, w_in, qkg, ones_bd, cs]
    out_specs = [
        pl.BlockSpec((TM, D_MODEL), lambda m, n: (m, 0)),
        pl.BlockSpec((TM, IN_TN), lambda m, n: (m, jnp.minimum(n, IN_QKV_BLKS - 1))),
        pl.BlockSpec((TM, IN_TN), lambda m, n: (m, u_blk(n))),
        pl.BlockSpec((TM, IN_TN), lambda m, n: (m, u_blk(n))),
    ]
    out_shape = [
        jax.ShapeDtypeStruct((N_TOK, D_MODEL), BF16),
        jax.ShapeDtypeStruct((N_TOK, 3 * D_NA), BF16),
        jax.ShapeDtypeStruct((N_TOK, D_F), BF16),
        jax.ShapeDtypeStruct((N_TOK, D_F), BF16),
    ]
    aliases = {}
    if with_kv:
        kv_spec = lambda first: pl.BlockSpec(
            (KV_TILE_B, None, SEQ, IN_TN),
            lambda m, n: (m, layer, 0, jnp.clip(n - first, 0, IN_Q_BLKS - 1)))
        out_specs += [kv_spec(IN_Q_BLKS), kv_spec(IN_QK_BLKS)]
        out_shape += [jax.ShapeDtypeStruct((BATCH, DEPTH, SEQ, D_NA), F32)] * 2
        in_specs += [pl.BlockSpec(memory_space=pl.ANY)] * 2
        aliases = {len(args): 4, len(args) + 1: 5}
        args += list(kv_prev)

    def body(*refs):
        _inproj_kernel(with_kv, *refs)

    return pl.pallas_call(
        body,
        grid=(N_MT, N_IN_BLK),
        in_specs=in_specs,
        out_specs=out_specs,
        out_shape=out_shape,
        scratch_shapes=[pltpu.VMEM((TM, 1), F32)],
        input_output_aliases=aliases,
        compiler_params=_params(2),
        name=f"in_projection_{path}_l{layer}",
    )(*args)


def _stack_heads(q):
    lane = lax.broadcasted_iota(jnp.int32, q.shape, 1)
    zero = jnp.zeros_like(q)
    return jnp.concatenate([jnp.where(lane < HEAD_DIM, q, zero),
                            jnp.where(lane >= HEAD_DIM, q, zero)], axis=0)


def _unstack_heads(o2):
    rows = o2.shape[0] // 2
    lane = lax.broadcasted_iota(jnp.int32, (rows, V7X_LANES), 1)
    return jnp.where(lane < HEAD_DIM, o2[:rows], o2[rows:])


def _qk(q2, k):
    return lax.dot_general(q2, k, (((1,), (1,)), ((), ())), preferred_element_type=F32)


def _softmax(scores):
    mx = scores[0].max(axis=-1, keepdims=True)
    for s in scores[1:]:
        mx = jnp.maximum(mx, s.max(axis=-1, keepdims=True))
    exps = [jnp.exp(s - mx) for s in scores]
    den = exps[0].sum(axis=-1, keepdims=True)
    for e in exps[1:]:
        den = den + e.sum(axis=-1, keepdims=True)
    inv = 1.0 / den
    return [(e * inv).astype(BF16) for e in exps]


CTX_STEP_B = 4


def _ctx_attn_kernel(q_ref, k_ref, v_ref, o_ref):
    for b in range(CTX_STEP_B):
        rows = slice(b * SEQ, (b + 1) * SEQ)
        for p in range(N_HEADS // 2):
            cols = slice(p * V7X_LANES, (p + 1) * V7X_LANES)
            q2 = _stack_heads(q_ref[rows, cols])
            prob, = _softmax([_qk(q2, k_ref[rows, cols])])
            o2 = _bdot(prob, v_ref[rows, cols])
            o_ref[rows, cols] = _unstack_heads(o2).astype(BF16)


def _context_attention(layer, p):
    blk = CTX_STEP_B * SEQ
    return pl.pallas_call(
        _ctx_attn_kernel,
        grid=(BATCH // CTX_STEP_B,),
        in_specs=[pl.BlockSpec((blk, D_NA), lambda b: (b, 0)),
                  pl.BlockSpec((blk, D_NA), lambda b: (b, 1)),
                  pl.BlockSpec((blk, D_NA), lambda b: (b, 2))],
        out_specs=pl.BlockSpec((blk, D_NA), lambda b: (b, 0)),
        out_shape=jax.ShapeDtypeStruct((N_TOK, D_NA), BF16),
        compiler_params=_params(1),
        name=f"context_attention_l{layer}",
    )(p, p, p)


NA_QROWS = 4
NA_KROWS = 12
NA_KSTART = (0, 0, 4, 4)


def _na_bias_strip(tbl_ref, group, strip):
    hh, rr = divmod(strip, NA_QROWS)
    ks = NA_KSTART[group]
    lane = lax.broadcasted_iota(jnp.int32, (GRID_W, V7X_LANES), 1)
    ninf = jnp.full((GRID_W, V7X_LANES), NEG_INF, F32)
    r = NA_QROWS * group + rr
    rs = min(max(r - KH // 2, 0), ROWS - KH)
    tiles = []
    for a in range(NA_KROWS // 2):
        kk0 = ks + 2 * a
        ok0 = rs <= kk0 < rs + KH
        ok1 = rs <= kk0 + 1 < rs + KH
        dr0 = kk0 - r + KH - 1
        if ok0 and ok1:
            t = tbl_ref[hh, dr0]
        elif ok0:
            t = jnp.where(lane < GRID_W, tbl_ref[hh, dr0], ninf)
        elif ok1:
            t = jnp.where(lane >= GRID_W, tbl_ref[hh, dr0], ninf)
        else:
            t = ninf
        tiles.append(t)
    return jnp.concatenate(tiles, axis=1)


NA_PAIRS = 4
NA_LANES = NA_PAIRS * V7X_LANES


def _lat_attn_kernel(q_ref, k_ref, v_ref, kc_ref, vc_ref, tbl_ref, o_ref):
    qrows = NA_QROWS * GRID_W
    for pr in range(NA_PAIRS):
        cols = slice(pr * V7X_LANES, (pr + 1) * V7X_LANES)
        kc = kc_ref[:, 2 * pr:2 * pr + 2, :].reshape(PAST_LEN, V7X_LANES).astype(BF16)
        vc = vc_ref[:, 2 * pr:2 * pr + 2, :].reshape(PAST_LEN, V7X_LANES).astype(BF16)
        tbl = tbl_ref.at[2 * pr:2 * pr + 2]
        for g in range(ROWS // NA_QROWS):
            k0 = NA_KSTART[g] * GRID_W
            q2 = _stack_heads(q_ref[g * qrows:(g + 1) * qrows, cols])
            bias = jnp.concatenate([_na_bias_strip(tbl, g, t) for t in range(2 * NA_QROWS)], axis=0)
            s_loc = _qk(q2, k_ref[k0:k0 + NA_KROWS * GRID_W, cols]) + bias
            p_loc, p_ctx = _softmax([s_loc, _qk(q2, kc)])
            o2 = _bdot(p_loc, v_ref[k0:k0 + NA_KROWS * GRID_W, cols]) + _bdot(p_ctx, vc)
            o_ref[g * qrows:(g + 1) * qrows, cols] = _unstack_heads(o2).astype(BF16)


def _latent_attention(layer, p, cache_k, cache_v, tbl):
    n_blk = D_NA // NA_LANES
    cache_spec = pl.BlockSpec((None, None, PAST_LEN, 2 * NA_PAIRS, HEAD_DIM),
                              lambda b, h: (b, layer, 0, h, 0))
    return pl.pallas_call(
        _lat_attn_kernel,
        grid=(DEC_BATCH, n_blk),
        in_specs=[
            pl.BlockSpec((DEC_SEQ, NA_LANES), lambda b, h: (b, h)),
            pl.BlockSpec((DEC_SEQ, NA_LANES), lambda b, h: (b, n_blk + h)),
            pl.BlockSpec((DEC_SEQ, NA_LANES), lambda b, h: (b, 2 * n_blk + h)),
            cache_spec, cache_spec,
            pl.BlockSpec((None, 2 * NA_PAIRS, N_DR, GRID_W, V7X_LANES), lambda b, h: (layer, h, 0, 0, 0)),
        ],
        out_specs=pl.BlockSpec((DEC_SEQ, NA_LANES), lambda b, h: (b, h)),
        out_shape=jax.ShapeDtypeStruct((N_TOK, D_NA), BF16),
        compiler_params=_params(2),
        name=f"latent_attention_l{layer}",
    )(p, p, p, cache_k, cache_v, tbl)


def _dft_tables(n, scale):
    j = np.arange(n, dtype=np.int64)
    ang = 2.0 * np.pi * ((j[:, None] * j[None, :]) % n).astype(np.float64) / n
    return (np.cos(ang) * scale).astype(np.float32), (np.sin(ang) * scale).astype(np.float32)


DFT_ROWS = 1024


def _pos_dft_kernel(seq, ct_ref, st_ref, ac_ref, as_ref, o_ref):
    ct = ct_ref[...]
    st = st_ref[...]
    for i in range(DFT_ROWS // seq):
        rows = slice(i * seq, (i + 1) * seq)
        o_ref[rows, :] = (_bdot(ct, ac_ref[rows, :]) - _bdot(st, as_ref[rows, :])).astype(BF16)


def _position_dft(layer, seq, a_cos, a_sin):
    ct, st = _dft_tables(seq, seq ** -0.5)
    blk = pl.BlockSpec((DFT_ROWS, D_F), lambda b: (b, 0))
    tab = pl.BlockSpec((seq, seq), lambda b: (0, 0))

    def body(*refs):
        _pos_dft_kernel(seq, *refs)

    return pl.pallas_call(
        body,
        grid=(N_TOK // DFT_ROWS,),
        in_specs=[tab, tab, blk, blk],
        out_specs=blk,
        out_shape=jax.ShapeDtypeStruct((N_TOK, D_F), BF16),
        compiler_params=_params(1),
        name=f"position_dft_{seq}_l{layer}",
    )(jnp.asarray(ct).astype(BF16), jnp.asarray(st).astype(BF16), a_cos, a_sin)


MIX_TC = 256
N_MIX_BLK = D_MODEL // MIX_TC
OUT_TN = 256
N_OUT_BLK = D_MODEL // OUT_TN


def _mix_kernel(h_ref, a_ref, f_ref, wgn_ref, wgf_ref, wna_ref, wf_ref, wo_ref, x_ref, g1_ref,
                o_ref, mix_s, wo_s, wna_s, wf_s):
    m = pl.program_id(0)
    s = pl.program_id(1)

    @pl.when(s < N_MIX_BLK)
    def _():
        @pl.when(m == 0)
        def _():
            wna_s[s] = wna_ref[...].astype(BF16)
            wf_s[s] = wf_ref[...].astype(BF16)

        h = h_ref[...]
        g_na = _sigmoid(_bdot(h, wgn_ref[...].astype(BF16)))
        g_fn = _sigmoid(_bdot(h, wgf_ref[...].astype(BF16)))
        na = _bdot(a_ref[...], wna_s[s])
        fn = _bdot(f_ref[...], wf_s[s])
        mix_s[s] = (g_na * na + g_fn * fn).astype(BF16)

    @pl.when(s >= N_MIX_BLK)
    def _():
        n = s - N_MIX_BLK

        @pl.when(m == 0)
        def _():
            wo_s[n] = wo_ref[...].astype(BF16)

        acc = None
        for c in range(N_MIX_BLK):
            t = _bdot(mix_s[c], wo_s[n, c * MIX_TC:(c + 1) * MIX_TC, :])
            acc = t if acc is None else acc + t
        o_ref[...] = x_ref[...] + g1_ref[...] * acc


def _token_mixing(layer, path, h, attn, f, x, mod, w_gate, w_na, w_f, w_o):
    mix_blk = lambda s: jnp.minimum(s, N_MIX_BLK - 1)
    out_blk = lambda s: jnp.clip(s - N_MIX_BLK, 0, N_OUT_BLK - 1)
    first_tile_blk = lambda m, s: jnp.where(m == 0, mix_blk(s), N_MIX_BLK - 1)
    return pl.pallas_call(
        _mix_kernel,
        grid=(N_MT, N_MIX_BLK + N_OUT_BLK),
        in_specs=[
            _tile_rows_spec(D_MODEL, N_MIX_BLK - 1),
            _tile_rows_spec(D_NA, N_MIX_BLK - 1),
            _tile_rows_spec(D_F, N_MIX_BLK - 1),
            pl.BlockSpec((None, D_MODEL, MIX_TC), lambda m, s: (layer, 0, mix_blk(s))),
            pl.BlockSpec((None, D_MODEL, MIX_TC), lambda m, s: (layer, 0, N_MIX_BLK + mix_blk(s))),
            pl.BlockSpec((None, D_NA, MIX_TC), lambda m, s: (layer, 0, first_tile_blk(m, s))),
            pl.BlockSpec((None, D_F, MIX_TC), lambda m, s: (layer, 0, first_tile_blk(m, s))),
            pl.BlockSpec((None, D_MODEL, OUT_TN),
                         lambda m, s: (layer, 0, jnp.where(m == 0, out_blk(s), N_OUT_BLK - 1))),
            pl.BlockSpec((TM, OUT_TN), lambda m, s: (m, out_blk(s))),
            _mod_spec(layer, path, 2, OUT_TN, out_blk),
        ],
        out_specs=pl.BlockSpec((TM, OUT_TN), lambda m, s: (m, out_blk(s))),
        out_shape=jax.ShapeDtypeStruct((N_TOK, D_MODEL), F32),
        scratch_shapes=[pltpu.VMEM((N_MIX_BLK, TM, MIX_TC), BF16),
                        pltpu.VMEM((N_OUT_BLK, D_MODEL, OUT_TN), BF16),
                        pltpu.VMEM((N_MIX_BLK, D_NA, MIX_TC), BF16),
                        pltpu.VMEM((N_MIX_BLK, D_F, MIX_TC), BF16)],
        compiler_params=_params(2),
        name=f"token_mixing_{path}_l{layer}",
    )(h, attn, f, w_gate, w_gate, w_na, w_f, w_o, x, mod)


FF_TC = 256
N_FF_BLK = D_FF // FF_TC
DOWN_TN = 256
N_DOWN_BLK = D_MODEL // DOWN_TN


def _ffn_kernel(xf_ref, ng_ref, sc_ref, sh_ref, wa_ref, wg_ref, wd_ref, x_ref, g2_ref, o_ref,
                act_s, h_s, inv_s):
    s = pl.program_id(1)

    def gated_chunk():
        h = h_s[...]
        g = _bdot(h, wg_ref[...].astype(BF16))
        gate = g * _sigmoid(g)
        a = _bdot(h, wa_ref[...].astype(BF16))
        act_s[s] = (gate * a).astype(BF16)

    @pl.when(s == 0)
    def _():
        _norm_mod_rows(xf_ref, ng_ref, sc_ref, sh_ref, h_s, inv_s, TM)
        gated_chunk()

    @pl.when(jnp.logical_and(s > 0, s < N_FF_BLK))
    def _():
        gated_chunk()

    @pl.when(s >= N_FF_BLK)
    def _():
        acc = None
        for j in range(N_FF_BLK):
            t = _bdot(act_s[j], wd_ref[j * FF_TC:(j + 1) * FF_TC, :].astype(BF16))
            acc = t if acc is None else acc + t
        o_ref[...] = x_ref[...] + g2_ref[...] * acc


def _ffn(layer, path, x, mod, norm_g, w_gate_up, w_down):
    ff_blk = lambda s: jnp.minimum(s, N_FF_BLK - 1)
    out_blk = lambda s: jnp.clip(s - N_FF_BLK, 0, N_DOWN_BLK - 1)
    return pl.pallas_call(
        _ffn_kernel,
        grid=(N_MT, N_FF_BLK + N_DOWN_BLK),
        in_specs=[
            _tile_rows_spec(D_MODEL, 0),
            pl.BlockSpec((None, 1, D_MODEL), lambda m, s: (layer, 0, 0)),
            _mod_spec(layer, path, 4, D_MODEL, lambda s: 0),
            _mod_spec(layer, path, 3, D_MODEL, lambda s: 0),
            pl.BlockSpec((None, D_MODEL, FF_TC), lambda m, s: (layer, 0, ff_blk(s))),
            pl.BlockSpec((None, D_MODEL, FF_TC), lambda m, s: (layer, 0, N_FF_BLK + ff_blk(s))),
            pl.BlockSpec((None, D_FF, DOWN_TN), lambda m, s: (layer, 0, out_blk(s))),
            pl.BlockSpec((TM, DOWN_TN), lambda m, s: (m, out_blk(s))),
            _mod_spec(layer, path, 5, DOWN_TN, out_blk),
        ],
        out_specs=pl.BlockSpec((TM, DOWN_TN), lambda m, s: (m, out_blk(s))),
        out_shape=jax.ShapeDtypeStruct((N_TOK, D_MODEL), F32),
        scratch_shapes=[pltpu.VMEM((N_FF_BLK, TM, FF_TC), BF16),
                        pltpu.VMEM((TM, D_MODEL), BF16),
                        pltpu.VMEM((TM, 1), F32)],
        compiler_params=_params(2),
        name=f"ffn_{path}_l{layer}",
    )(x, norm_g.reshape(DEPTH, 1, D_MODEL), mod, mod, w_gate_up, w_gate_up, w_down, x, mod)


def kernel(x_prompt, x_sample, cache_k, cache_v, c, c_ctx, w_mod, b_mod, norm1_g, norm2_g,
           w_in, q_norm_g, k_norm_g, rpb, w_na_proj, w_fnet_proj, w_gate, w_o, w_gate_up, w_down):
    xs = {CTX: x_prompt.reshape(N_TOK, D_MODEL), LAT: x_sample.reshape(N_TOK, D_MODEL)}
    cond8 = jnp.concatenate([c_ctx[None, :], c, jnp.zeros((8 - 1 - DEC_BATCH, D_MODEL), F32)], axis=0)
    mod = _modulation(cond8, w_mod, b_mod).reshape(DEPTH, 8, 6, 1, D_MODEL)
    tbl = _bias_tables(rpb)

    reps = IN_TN // HEAD_DIM
    qkg = jnp.stack([jnp.tile(q_norm_g * (HEAD_DIM ** -0.5), (1, reps)),
                     jnp.tile(k_norm_g, (1, reps))], axis=1).reshape(DEPTH, 2, 1, IN_TN)
    head_id = np.arange(QK_SS_W) // HEAD_DIM
    ones_bd = jnp.asarray((head_id[:, None] == head_id[None, :]).astype(np.float32)).astype(BF16)
    cc, sc = _dft_tables(F_GROUP, F_GROUP ** -0.5)
    cs = jnp.asarray(np.concatenate([cc, sc], axis=1)).astype(BF16)

    ck, cv = cache_k, cache_v

    kv = tuple(jnp.zeros((BATCH, DEPTH, SEQ, D_NA), F32) for _ in range(2))
    for l in range(DEPTH):
        for path in (CTX, LAT):
            x = xs[path]
            outs = _in_projection(l, path, x, mod, norm1_g, w_in, qkg, ones_bd, cs, kv_prev=kv)
            h, p, a_cos, a_sin = outs[:4]
            if path == CTX:
                kv = outs[4:]
                attn = _context_attention(l, p)
                f = _position_dft(l, SEQ, a_cos, a_sin)
            else:
                attn = _latent_attention(l, p, ck, cv, tbl)
                f = _position_dft(l, DEC_SEQ, a_cos, a_sin)
            x1 = _token_mixing(l, path, h, attn, f, x, mod, w_gate, w_na_proj, w_fnet_proj, w_o)
            xs[path] = _ffn(l, path, x1, mod, norm2_g, w_gate_up, w_down)

    new_k, new_v = (t.reshape(BATCH, DEPTH, SEQ, N_HEADS, HEAD_DIM) for t in kv)
    return (xs[CTX].reshape(BATCH, SEQ, D_MODEL), xs[LAT].reshape(DEC_BATCH, DEC_SEQ, D_MODEL),
            new_k, new_v)
```

```python
import numpy as np
import jax
import jax.numpy as jnp
from jax import lax
from jax.experimental import pallas as pl
from jax.experimental.pallas import tpu as pltpu

F32 = jnp.float32
BF16 = jnp.bfloat16

D_MODEL = 2048
BATCH = 16
SEQ = 256
DEPTH = 2
DEC_BATCH = 4
DEC_SEQ = 1024
PAST_LEN = 256
GRID_W = 64
ROWS = DEC_SEQ // GRID_W
N_HEADS = 16
HEAD_DIM = 64
D_NA = N_HEADS * HEAD_DIM
D_F = D_MODEL // 2
N_FGROUPS = 4
F_GROUP = D_F // N_FGROUPS
KH = 8
KW = 16
D_FF = 5632
D_IN = 3 * D_NA + D_F
EPS = 1e-6
N_DR = 2 * KH - 1
N_DC = 2 * KW - 1

N_TOK = BATCH * SEQ
assert N_TOK == DEC_BATCH * DEC_SEQ

V7X_LANES = 128
V7X_VMEM_LIMIT = 60 * 1024 * 1024

TM = 1024
N_MT = N_TOK // TM
NEG_INF = float("-inf")
CTX, LAT = "ctx", "lat"


def _params(n_axes):
    return pltpu.CompilerParams(dimension_semantics=("arbitrary",) * n_axes,
                                vmem_limit_bytes=V7X_VMEM_LIMIT)


def _mod_row(path, m, tile):
    return 0 if path == CTX else 1 + m // (DEC_SEQ // tile)


def _mod_spec(layer, path, chunk, width, col_map):
    return pl.BlockSpec((None, None, None, 1, width),
                        lambda m, s: (layer, _mod_row(path, m, TM), chunk, 0, col_map(s)))


def _tile_rows_spec(width, last_use):
    def index(m, s):
        return (jnp.minimum(m + jnp.where(s > last_use, 1, 0), N_MT - 1), 0)
    return pl.BlockSpec((TM, width), index)


def _sigmoid(z):
    return 1.0 / (1.0 + jnp.exp(-z))


def _bdot(a, b):
    return jnp.dot(a, b, preferred_element_type=F32)


MOD_TN = 1024


def _mod_kernel(cond_ref, w_ref, b_ref, o_ref):
    cnd = cond_ref[...]
    s = (cnd * _sigmoid(cnd)).astype(BF16)
    o_ref[...] = _bdot(s, w_ref[...].astype(BF16)) + b_ref[...]


def _modulation(cond8, w_mod, b_mod):
    n_blk = 6 * D_MODEL // MOD_TN
    return pl.pallas_call(
        _mod_kernel,
        grid=(DEPTH, n_blk),
        in_specs=[
            pl.BlockSpec((8, D_MODEL), lambda l, n: (0, 0)),
            pl.BlockSpec((None, D_MODEL, MOD_TN), lambda l, n: (l, 0, n)),
            pl.BlockSpec((None, 1, MOD_TN), lambda l, n: (l, 0, n)),
        ],
        out_specs=pl.BlockSpec((None, 8, MOD_TN), lambda l, n: (l, 0, n)),
        out_shape=jax.ShapeDtypeStruct((DEPTH, 8, 6 * D_MODEL), F32),
        compiler_params=_params(2),
        name="modulation",
    )(cond8, w_mod, b_mod.reshape(DEPTH, 1, 6 * D_MODEL))


def _bias_onehots():
    width = GRID_W * V7X_LANES
    oh_l = np.zeros((N_DC + 1, width), np.float32)
    oh_r = np.zeros((N_DC + 1, width), np.float32)
    mask = np.full((1, width), -np.inf, np.float32)
    for c in range(GRID_W):
        ws = min(max(c - KW // 2, 0), GRID_W - KW)
        for kc in range(ws, ws + KW):
            e = kc - c + KW - 1
            oh_l[e, c * V7X_LANES + kc] = 1.0
            oh_r[e, c * V7X_LANES + GRID_W + kc] = 1.0
            mask[0, c * V7X_LANES + kc] = 0.0
            mask[0, c * V7X_LANES + GRID_W + kc] = 0.0
    mask_last = mask.copy().reshape(GRID_W, V7X_LANES)
    mask_last[:, GRID_W:] = -np.inf
    return oh_l, oh_r, mask, mask_last.reshape(1, width)


def _split3(x):
    hi = x.astype(BF16)
    r1 = x - hi.astype(F32)
    mid = r1.astype(BF16)
    lo = (r1 - mid.astype(F32)).astype(BF16)
    return hi, mid, lo


def _bias_kernel(r1_ref, r2_ref, sel_ref, ohl_ref, ohr_ref, mask_ref, maskl_ref, o_ref):
    acc = None
    for r_ref, oh_ref in ((r1_ref, ohl_ref), (r2_ref, ohr_ref)):
        oh = oh_ref[...]
        for piece in _split3(r_ref[...]):
            t = _bdot(piece, oh)
            acc = t if acc is None else acc + t
    mask = jnp.where(sel_ref[...] > 0.5, maskl_ref[...], mask_ref[...])
    o_ref[...] = (acc + mask).reshape(o_ref.shape)


def _bias_tables(rpb):
    oh_l, oh_r, mask, mask_last = _bias_onehots()
    rows = N_HEADS * N_DR
    width = GRID_W * V7X_LANES
    pad = jnp.zeros((DEPTH, N_HEADS, N_DR, 1), F32)
    r1 = jnp.concatenate([rpb, pad], axis=-1).reshape(DEPTH * rows, N_DC + 1)
    nxt = jnp.concatenate([rpb[:, :, 1:], jnp.zeros((DEPTH, N_HEADS, 1, N_DC), F32)], axis=2)
    r2 = jnp.concatenate([nxt, pad], axis=-1).reshape(DEPTH * rows, N_DC + 1)
    sel = np.zeros((DEPTH, N_HEADS, N_DR, 1), np.float32)
    sel[:, :, N_DR - 1] = 1.0
    sel = jnp.asarray(sel.reshape(DEPTH * rows, 1))
    const = lambda shape: pl.BlockSpec(shape, lambda l: (0, 0))
    out = pl.pallas_call(
        _bias_kernel,
        grid=(DEPTH,),
        in_specs=[
            pl.BlockSpec((rows, N_DC + 1), lambda l: (l, 0)),
            pl.BlockSpec((rows, N_DC + 1), lambda l: (l, 0)),
            pl.BlockSpec((rows, 1), lambda l: (l, 0)),
            const((N_DC + 1, width)), const((N_DC + 1, width)),
            const((1, width)), const((1, width)),
        ],
        out_specs=pl.BlockSpec((rows, GRID_W, V7X_LANES), lambda l: (l, 0, 0)),
        out_shape=jax.ShapeDtypeStruct((DEPTH * rows, GRID_W, V7X_LANES), F32),
        compiler_params=_params(1),
        name="bias_tables",
    )(r1, r2, sel, jnp.asarray(oh_l).astype(BF16), jnp.asarray(oh_r).astype(BF16),
      jnp.asarray(mask), jnp.asarray(mask_last))
    return out.reshape(DEPTH, N_HEADS, N_DR, GRID_W, V7X_LANES)


NORM_RC = 128


def _norm_mod_rows(x_ref, g_ref, sc_ref, sh_ref, dst_ref, inv_ref, rows):
    def stats(i, carry):
        r0 = pl.multiple_of(i * NORM_RC, NORM_RC)
        x = x_ref[pl.ds(r0, NORM_RC), :]
        inv_ref[pl.ds(r0, NORM_RC), :] = lax.rsqrt(jnp.mean(x * x, axis=-1, keepdims=True) + EPS)
        return carry

    lax.fori_loop(0, rows // NORM_RC, stats, 0, unroll=True)
    gain = g_ref[...]
    scale1 = 1.0 + sc_ref[...]
    shift = sh_ref[...]

    def apply(i, carry):
        r0 = pl.multiple_of(i * NORM_RC, NORM_RC)
        y = x_ref[pl.ds(r0, NORM_RC), :] * inv_ref[pl.ds(r0, NORM_RC), :]
        dst_ref[pl.ds(r0, NORM_RC), :] = ((y * gain) * scale1 + shift).astype(BF16)
        return carry

    lax.fori_loop(0, rows // NORM_RC, apply, 0, unroll=True)


IN_TN = 512
N_IN_BLK = D_IN // IN_TN
IN_Q_BLKS = D_NA // IN_TN
IN_QK_BLKS = 2 * IN_Q_BLKS
IN_QKV_BLKS = 3 * IN_Q_BLKS
KV_TILE_B = TM // SEQ
QK_SS_W = 256


def _inproj_kernel(with_kv, x_ref, g_ref, sc_ref, sh_ref, w_ref, qkg_ref, ones_ref, cs_ref, *rest):
    if with_kv:
        h_ref, p_ref, ac_ref, as_ref, k_ref, v_ref, inv_s = rest[-7:]
    else:
        h_ref, p_ref, ac_ref, as_ref, inv_s = rest[-5:]
    n = pl.program_id(1)

    def project():
        return _bdot(h_ref[...], w_ref[...].astype(BF16))

    def finish_qk(y):
        y2 = (y * y).astype(BF16)
        ones = ones_ref[...]
        ss = jnp.concatenate([_bdot(y2[:, i * QK_SS_W:(i + 1) * QK_SS_W], ones)
                              for i in range(IN_TN // QK_SS_W)], axis=1)
        yn = y * lax.rsqrt(ss * (1.0 / HEAD_DIM) + EPS) * qkg_ref[...]
        p_ref[...] = yn.astype(BF16)
        if with_kv:
            @pl.when(n >= IN_Q_BLKS)
            def _():
                k_ref[...] = yn.reshape(KV_TILE_B, SEQ, IN_TN)

    def finish_v(y):
        p_ref[...] = y.astype(BF16)
        if with_kv:
            v_ref[...] = y.reshape(KV_TILE_B, SEQ, IN_TN)

    def finish_u(y):
        cs = cs_ref[...]
        for gg in range(IN_TN // F_GROUP):
            a = _bdot(y[:, gg * F_GROUP:(gg + 1) * F_GROUP].astype(BF16), cs)
            ac_ref[:, gg * F_GROUP:(gg + 1) * F_GROUP] = a[:, :F_GROUP].astype(BF16)
            as_ref[:, gg * F_GROUP:(gg + 1) * F_GROUP] = a[:, F_GROUP:].astype(BF16)

    @pl.when(n == 0)
    def _():
        _norm_mod_rows(x_ref, g_ref, sc_ref, sh_ref, h_ref, inv_s, TM)
        finish_qk(project())

    @pl.when(n > 0)
    def _():
        y = project()

        @pl.when(n < IN_QK_BLKS)
        def _():
            finish_qk(y)

        @pl.when(jnp.logical_and(n >= IN_QK_BLKS, n < IN_QKV_BLKS))
        def _():
            finish_v(y)

        @pl.when(n >= IN_QKV_BLKS)
        def _():
            finish_u(y)


def _in_projection(layer, path, x, mod, norm_g, w_in, qkg, ones_bd, cs, kv_prev=None):
    with_kv = path == CTX
    u_blk = lambda n: jnp.clip(n - IN_QKV_BLKS, 0, 1)
    in_specs = [
        _tile_rows_spec(D_MODEL, 0),
        pl.BlockSpec((None, 1, D_MODEL), lambda m, n: (layer, 0, 0)),
        _mod_spec(layer, path, 1, D_MODEL, lambda n: 0),
        _mod_spec(layer, path, 0, D_MODEL, lambda n: 0),
        pl.BlockSpec((None, D_MODEL, IN_TN), lambda m, n: (layer, 0, n)),
        pl.BlockSpec((None, None, 1, IN_TN),
                     lambda m, n: (layer, jnp.clip(n // IN_Q_BLKS, 0, 1), 0, 0)),
        pl.BlockSpec((QK_SS_W, QK_SS_W), lambda m, n: (0, 0)),
        pl.BlockSpec((F_GROUP, 2 * F_GROUP), lambda m, n: (0, 0)),
    ]
    args = [x, norm_g.reshape(DEPTH, 1, D_MODEL), mod, mod, w_in, qkg, ones_bd, cs]
    out_specs = [
        pl.BlockSpec((TM, D_MODEL), lambda m, n: (m, 0)),
        pl.BlockSpec((TM, IN_TN), lambda m, n: (m, jnp.minimum(n, IN_QKV_BLKS - 1))),
        pl.BlockSpec((TM, IN_TN), lambda m, n: (m, u_blk(n))),
        pl.BlockSpec((TM, IN_TN), lambda m, n: (m, u_blk(n))),
    ]
    out_shape = [
        jax.ShapeDtypeStruct((N_TOK, D_MODEL), BF16),
        jax.ShapeDtypeStruct((N_TOK, 3 * D_NA), BF16),
        jax.ShapeDtypeStruct((N_TOK, D_F), BF16),
        jax.ShapeDtypeStruct((N_TOK, D_F), BF16),
    ]
    aliases = {}
    if with_kv:
        kv_spec = lambda first: pl.BlockSpec(
            (KV_TILE_B, None, SEQ, IN_TN),
            lambda m, n: (m, layer, 0, jnp.clip(n - first, 0, IN_Q_BLKS - 1)))
        out_specs += [kv_spec(IN_Q_BLKS), kv_spec(IN_QK_BLKS)]
        out_shape += [jax.ShapeDtypeStruct((BATCH, DEPTH, SEQ, D_NA), F32)] * 2
        in_specs += [pl.BlockSpec(memory_space=pl.ANY)] * 2
        aliases = {len(args): 4, len(args) + 1: 5}
        args += list(kv_prev)

    def body(*refs):
        _inproj_kernel(with_kv, *refs)

    return pl.pallas_call(
        body,
        grid=(N_MT, N_IN_BLK),
        in_specs=in_specs,
        out_specs=out_specs,
        out_shape=out_shape,
        scratch_shapes=[pltpu.VMEM((TM, 1), F32)],
        input_output_aliases=aliases,
        compiler_params=_params(2),
        name=f"in_projection_{path}_l{layer}",
    )(*args)


def _stack_heads(q):
    lane = lax.broadcasted_iota(jnp.int32, q.shape, 1)
    zero = jnp.zeros_like(q)
    return jnp.concatenate([jnp.where(lane < HEAD_DIM, q, zero),
                            jnp.where(lane >= HEAD_DIM, q, zero)], axis=0)


def _unstack_heads(o2):
    rows = o2.shape[0] // 2
    lane = lax.broadcasted_iota(jnp.int32, (rows, V7X_LANES), 1)
    return jnp.where(lane < HEAD_DIM, o2[:rows], o2[rows:])


def _qk(q2, k):
    return lax.dot_general(q2, k, (((1,), (1,)), ((), ())), preferred_element_type=F32)


def _softmax(scores):
    mx = scores[0].max(axis=-1, keepdims=True)
    for s in scores[1:]:
        mx = jnp.maximum(mx, s.max(axis=-1, keepdims=True))
    exps = [jnp.exp(s - mx) for s in scores]
    den = exps[0].sum(axis=-1, keepdims=True)
    for e in exps[1:]:
        den = den + e.sum(axis=-1, keepdims=True)
    inv = 1.0 / den
    return [(e * inv).astype(BF16) for e in exps]


CTX_STEP_B = 4


def _ctx_attn_kernel(q_ref, k_ref, v_ref, ct_ref, st_ref, ac_ref, as_ref, o_ref, f_ref):
    ct = ct_ref[...]
    st = st_ref[...]
    for b in range(CTX_STEP_B):
        rows = slice(b * SEQ, (b + 1) * SEQ)
        for p in range(N_HEADS // 2):
            cols = slice(p * V7X_LANES, (p + 1) * V7X_LANES)
            q2 = _stack_heads(q_ref[rows, cols])
            prob, = _softmax([_qk(q2, k_ref[rows, cols])])
            o2 = _bdot(prob, v_ref[rows, cols])
            o_ref[rows, cols] = _unstack_heads(o2).astype(BF16)
        f_ref[rows, :] = (_bdot(ct, ac_ref[rows, :]) - _bdot(st, as_ref[rows, :])).astype(BF16)


def _context_attention(layer, p, a_cos, a_sin):
    blk = CTX_STEP_B * SEQ
    ct, st = _dft_tables(SEQ, SEQ ** -0.5)
    tab = pl.BlockSpec((SEQ, SEQ), lambda b: (0, 0))
    return pl.pallas_call(
        _ctx_attn_kernel,
        grid=(BATCH // CTX_STEP_B,),
        in_specs=[pl.BlockSpec((blk, D_NA), lambda b: (b, 0)),
                  pl.BlockSpec((blk, D_NA), lambda b: (b, 1)),
                  pl.BlockSpec((blk, D_NA), lambda b: (b, 2)),
                  tab, tab,
                  pl.BlockSpec((blk, D_F), lambda b: (b, 0)),
                  pl.BlockSpec((blk, D_F), lambda b: (b, 0))],
        out_specs=[pl.BlockSpec((blk, D_NA), lambda b: (b, 0)),
                   pl.BlockSpec((blk, D_F), lambda b: (b, 0))],
        out_shape=[jax.ShapeDtypeStruct((N_TOK, D_NA), BF16),
                   jax.ShapeDtypeStruct((N_TOK, D_F), BF16)],
        compiler_params=_params(1),
        name=f"context_attention_dft_l{layer}",
    )(p, p, p, jnp.asarray(ct).astype(BF16), jnp.asarray(st).astype(BF16), a_cos, a_sin)


NA_QROWS = 4
NA_KROWS = 12
NA_KSTART = (0, 0, 4, 4)


def _na_bias_strip(tbl_ref, group, strip):
    hh, rr = divmod(strip, NA_QROWS)
    ks = NA_KSTART[group]
    lane = lax.broadcasted_iota(jnp.int32, (GRID_W, V7X_LANES), 1)
    ninf = jnp.full((GRID_W, V7X_LANES), NEG_INF, F32)
    r = NA_QROWS * group + rr
    rs = min(max(r - KH // 2, 0), ROWS - KH)
    tiles = []
    for a in range(NA_KROWS // 2):
        kk0 = ks + 2 * a
        ok0 = rs <= kk0 < rs + KH
        ok1 = rs <= kk0 + 1 < rs + KH
        dr0 = kk0 - r + KH - 1
        if ok0 and ok1:
            t = tbl_ref[hh, dr0]
        elif ok0:
            t = jnp.where(lane < GRID_W, tbl_ref[hh, dr0], ninf)
        elif ok1:
            t = jnp.where(lane >= GRID_W, tbl_ref[hh, dr0], ninf)
        else:
            t = ninf
        tiles.append(t)
    return jnp.concatenate(tiles, axis=1)


NA_PAIRS = 4
NA_LANES = NA_PAIRS * V7X_LANES


def _lat_attn_kernel(q_ref, k_ref, v_ref, kc_ref, vc_ref, tbl_ref, o_ref):
    qrows = NA_QROWS * GRID_W
    for pr in range(NA_PAIRS):
        cols = slice(pr * V7X_LANES, (pr + 1) * V7X_LANES)
        kc = kc_ref[:, cols].astype(BF16)
        vc = vc_ref[:, cols].astype(BF16)
        tbl = tbl_ref.at[2 * pr:2 * pr + 2]
        for g in range(ROWS // NA_QROWS):
            k0 = NA_KSTART[g] * GRID_W
            q2 = _stack_heads(q_ref[g * qrows:(g + 1) * qrows, cols])
            bias = jnp.concatenate([_na_bias_strip(tbl, g, t) for t in range(2 * NA_QROWS)], axis=0)
            s_loc = _qk(q2, k_ref[k0:k0 + NA_KROWS * GRID_W, cols]) + bias
            p_loc, p_ctx = _softmax([s_loc, _qk(q2, kc)])
            o2 = _bdot(p_loc, v_ref[k0:k0 + NA_KROWS * GRID_W, cols]) + _bdot(p_ctx, vc)
            o_ref[g * qrows:(g + 1) * qrows, cols] = _unstack_heads(o2).astype(BF16)


def _latent_attention(layer, p, cache_k, cache_v, tbl):
    n_blk = D_NA // NA_LANES
    cache_spec = pl.BlockSpec((None, None, PAST_LEN, NA_LANES), lambda b, h: (b, layer, 0, h))
    return pl.pallas_call(
        _lat_attn_kernel,
        grid=(DEC_BATCH, n_blk),
        in_specs=[
            pl.BlockSpec((DEC_SEQ, NA_LANES), lambda b, h: (b, h)),
            pl.BlockSpec((DEC_SEQ, NA_LANES), lambda b, h: (b, n_blk + h)),
            pl.BlockSpec((DEC_SEQ, NA_LANES), lambda b, h: (b, 2 * n_blk + h)),
            cache_spec, cache_spec,
            pl.BlockSpec((None, 2 * NA_PAIRS, N_DR, GRID_W, V7X_LANES), lambda b, h: (layer, h, 0, 0, 0)),
        ],
        out_specs=pl.BlockSpec((DEC_SEQ, NA_LANES), lambda b, h: (b, h)),
        out_shape=jax.ShapeDtypeStruct((N_TOK, D_NA), BF16),
        compiler_params=_params(2),
        name=f"latent_attention_l{layer}",
    )(p, p, p, cache_k, cache_v, tbl)


def _dft_tables(n, scale):
    j = np.arange(n, dtype=np.int64)
    ang = 2.0 * np.pi * ((j[:, None] * j[None, :]) % n).astype(np.float64) / n
    return (np.cos(ang) * scale).astype(np.float32), (np.sin(ang) * scale).astype(np.float32)


DFT_ROWS = 1024


def _pos_dft_kernel(seq, ct_ref, st_ref, ac_ref, as_ref, o_ref):
    ct = ct_ref[...]
    st = st_ref[...]
    for i in range(DFT_ROWS // seq):
        rows = slice(i * seq, (i + 1) * seq)
        o_ref[rows, :] = (_bdot(ct, ac_ref[rows, :]) - _bdot(st, as_ref[rows, :])).astype(BF16)


def _position_dft(layer, seq, a_cos, a_sin):
    ct, st = _dft_tables(seq, seq ** -0.5)
    blk = pl.BlockSpec((DFT_ROWS, D_F), lambda b: (b, 0))
    tab = pl.BlockSpec((seq, seq), lambda b: (0, 0))

    def body(*refs):
        _pos_dft_kernel(seq, *refs)

    return pl.pallas_call(
        body,
        grid=(N_TOK // DFT_ROWS,),
        in_specs=[tab, tab, blk, blk],
        out_specs=blk,
        out_shape=jax.ShapeDtypeStruct((N_TOK, D_F), BF16),
        compiler_params=_params(1),
        name=f"position_dft_{seq}_l{layer}",
    )(jnp.asarray(ct).astype(BF16), jnp.asarray(st).astype(BF16), a_cos, a_sin)


MIX_TC = 256
N_MIX_BLK = D_MODEL // MIX_TC
OUT_TN = 256
N_OUT_BLK = D_MODEL // OUT_TN


def _mix_kernel(h_ref, a_ref, f_ref, wgn_ref, wgf_ref, wna_ref, wf_ref, wo_ref, x_ref, g1_ref,
                o_ref, mix_s, wo_s, wna_s, wf_s):
    m = pl.program_id(0)
    s = pl.program_id(1)

    @pl.when(s < N_MIX_BLK)
    def _():
        @pl.when(m == 0)
        def _():
            wna_s[s] = wna_ref[...].astype(BF16)
            wf_s[s] = wf_ref[...].astype(BF16)

        h = h_ref[...]
        g_na = _sigmoid(_bdot(h, wgn_ref[...].astype(BF16)))
        g_fn = _sigmoid(_bdot(h, wgf_ref[...].astype(BF16)))
        na = _bdot(a_ref[...], wna_s[s])
        fn = _bdot(f_ref[...], wf_s[s])
        mix_s[s] = (g_na * na + g_fn * fn).astype(BF16)

    @pl.when(s >= N_MIX_BLK)
    def _():
        n = s - N_MIX_BLK

        @pl.when(m == 0)
        def _():
            wo_s[n] = wo_ref[...].astype(BF16)

        acc = None
        for c in range(N_MIX_BLK):
            t = _bdot(mix_s[c], wo_s[n, c * MIX_TC:(c + 1) * MIX_TC, :])
            acc = t if acc is None else acc + t
        o_ref[...] = x_ref[...] + g1_ref[...] * acc


def _token_mixing(layer, path, h, attn, f, x, mod, w_gate, w_na, w_f, w_o):
    mix_blk = lambda s: jnp.minimum(s, N_MIX_BLK - 1)
    out_blk = lambda s: jnp.clip(s - N_MIX_BLK, 0, N_OUT_BLK - 1)
    first_tile_blk = lambda m, s: jnp.where(m == 0, mix_blk(s), N_MIX_BLK - 1)
    return pl.pallas_call(
        _mix_kernel,
        grid=(N_MT, N_MIX_BLK + N_OUT_BLK),
        in_specs=[
            _tile_rows_spec(D_MODEL, N_MIX_BLK - 1),
            _tile_rows_spec(D_NA, N_MIX_BLK - 1),
            _tile_rows_spec(D_F, N_MIX_BLK - 1),
            pl.BlockSpec((None, D_MODEL, MIX_TC), lambda m, s: (layer, 0, mix_blk(s))),
            pl.BlockSpec((None, D_MODEL, MIX_TC), lambda m, s: (layer, 0, N_MIX_BLK + mix_blk(s))),
            pl.BlockSpec((None, D_NA, MIX_TC), lambda m, s: (layer, 0, first_tile_blk(m, s))),
            pl.BlockSpec((None, D_F, MIX_TC), lambda m, s: (layer, 0, first_tile_blk(m, s))),
            pl.BlockSpec((None, D_MODEL, OUT_TN),
                         lambda m, s: (layer, 0, jnp.where(m == 0, out_blk(s), N_OUT_BLK - 1))),
            pl.BlockSpec((TM, OUT_TN), lambda m, s: (m, out_blk(s))),
            _mod_spec(layer, path, 2, OUT_TN, out_blk),
        ],
        out_specs=pl.BlockSpec((TM, OUT_TN), lambda m, s: (m, out_blk(s))),
        out_shape=jax.ShapeDtypeStruct((N_TOK, D_MODEL), F32),
        scratch_shapes=[pltpu.VMEM((N_MIX_BLK, TM, MIX_TC), BF16),
                        pltpu.VMEM((N_OUT_BLK, D_MODEL, OUT_TN), BF16),
                        pltpu.VMEM((N_MIX_BLK, D_NA, MIX_TC), BF16),
                        pltpu.VMEM((N_MIX_BLK, D_F, MIX_TC), BF16)],
        compiler_params=_params(2),
        name=f"token_mixing_{path}_l{layer}",
    )(h, attn, f, w_gate, w_gate, w_na, w_f, w_o, x, mod)


FF_TC = 256
N_FF_BLK = D_FF // FF_TC
DOWN_TN = 256
N_DOWN_BLK = D_MODEL // DOWN_TN


def _ffn_kernel(xf_ref, ng_ref, sc_ref, sh_ref, wa_ref, wg_ref, wd_ref, x_ref, g2_ref, o_ref,
                act_s, h_s, inv_s):
    s = pl.program_id(1)

    def gated_chunk():
        h = h_s[...]
        g = _bdot(h, wg_ref[...].astype(BF16))
        gate = g * _sigmoid(g)
        a = _bdot(h, wa_ref[...].astype(BF16))
        act_s[s] = (gate * a).astype(BF16)

    @pl.when(s == 0)
    def _():
        _norm_mod_rows(xf_ref, ng_ref, sc_ref, sh_ref, h_s, inv_s, TM)
        gated_chunk()

    @pl.when(jnp.logical_and(s > 0, s < N_FF_BLK))
    def _():
        gated_chunk()

    @pl.when(s >= N_FF_BLK)
    def _():
        acc = None
        for j in range(N_FF_BLK):
            t = _bdot(act_s[j], wd_ref[j * FF_TC:(j + 1) * FF_TC, :].astype(BF16))
            acc = t if acc is None else acc + t
        o_ref[...] = x_ref[...] + g2_ref[...] * acc


def _ffn(layer, path, x, mod, norm_g, w_gate_up, w_down):
    ff_blk = lambda s: jnp.minimum(s, N_FF_BLK - 1)
    out_blk = lambda s: jnp.clip(s - N_FF_BLK, 0, N_DOWN_BLK - 1)
    return pl.pallas_call(
        _ffn_kernel,
        grid=(N_MT, N_FF_BLK + N_DOWN_BLK),
        in_specs=[
            _tile_rows_spec(D_MODEL, 0),
            pl.BlockSpec((None, 1, D_MODEL), lambda m, s: (layer, 0, 0)),
            _mod_spec(layer, path, 4, D_MODEL, lambda s: 0),
            _mod_spec(layer, path, 3, D_MODEL, lambda s: 0),
            pl.BlockSpec((None, D_MODEL, FF_TC), lambda m, s: (layer, 0, ff_blk(s))),
            pl.BlockSpec((None, D_MODEL, FF_TC), lambda m, s: (layer, 0, N_FF_BLK + ff_blk(s))),
            pl.BlockSpec((None, D_FF, DOWN_TN), lambda m, s: (layer, 0, out_blk(s))),
            pl.BlockSpec((TM, DOWN_TN), lambda m, s: (m, out_blk(s))),
            _mod_spec(layer, path, 5, DOWN_TN, out_blk),
        ],
        out_specs=pl.BlockSpec((TM, DOWN_TN), lambda m, s: (m, out_blk(s))),
        out_shape=jax.ShapeDtypeStruct((N_TOK, D_MODEL), F32),
        scratch_shapes=[pltpu.VMEM((N_FF_BLK, TM, FF_TC), BF16),
                        pltpu.VMEM((TM, D_MODEL), BF16),
                        pltpu.VMEM((TM, 1), F32)],
        compiler_params=_params(2),
        name=f"ffn_{path}_l{layer}",
    )(x, norm_g.reshape(DEPTH, 1, D_MODEL), mod, mod, w_gate_up, w_gate_up, w_down, x, mod)


def kernel(x_prompt, x_sample, cache_k, cache_v, c, c_ctx, w_mod, b_mod, norm1_g, norm2_g,
           w_in, q_norm_g, k_norm_g, rpb, w_na_proj, w_fnet_proj, w_gate, w_o, w_gate_up, w_down):
    xs = {CTX: x_prompt.reshape(N_TOK, D_MODEL), LAT: x_sample.reshape(N_TOK, D_MODEL)}
    cond8 = jnp.concatenate([c_ctx[None, :], c, jnp.zeros((8 - 1 - DEC_BATCH, D_MODEL), F32)], axis=0)
    mod = _modulation(cond8, w_mod, b_mod).reshape(DEPTH, 8, 6, 1, D_MODEL)
    tbl = _bias_tables(rpb)

    reps = IN_TN // HEAD_DIM
    qkg = jnp.stack([jnp.tile(q_norm_g * (HEAD_DIM ** -0.5), (1, reps)),
                     jnp.tile(k_norm_g, (1, reps))], axis=1).reshape(DEPTH, 2, 1, IN_TN)
    head_id = np.arange(QK_SS_W) // HEAD_DIM
    ones_bd = jnp.asarray((head_id[:, None] == head_id[None, :]).astype(np.float32)).astype(BF16)
    cc, sc = _dft_tables(F_GROUP, F_GROUP ** -0.5)
    cs = jnp.asarray(np.concatenate([cc, sc], axis=1)).astype(BF16)

    ck = cache_k.reshape(DEC_BATCH, DEPTH, PAST_LEN, D_NA)
    cv = cache_v.reshape(DEC_BATCH, DEPTH, PAST_LEN, D_NA)

    kv = tuple(jnp.zeros((BATCH, DEPTH, SEQ, D_NA), F32) for _ in range(2))
    for l in range(DEPTH):
        for path in (CTX, LAT):
            x = xs[path]
            outs = _in_projection(l, path, x, mod, norm1_g, w_in, qkg, ones_bd, cs, kv_prev=kv)
            h, p, a_cos, a_sin = outs[:4]
            if path == CTX:
                kv = outs[4:]
                attn, f = _context_attention(l, p, a_cos, a_sin)
            else:
                attn = _latent_attention(l, p, ck, cv, tbl)
                f = _position_dft(l, DEC_SEQ, a_cos, a_sin)
            x1 = _token_mixing(l, path, h, attn, f, x, mod, w_gate, w_na_proj, w_fnet_proj, w_o)
            xs[path] = _ffn(l, path, x1, mod, norm2_g, w_gate_up, w_down)

    new_k, new_v = (t.reshape(BATCH, DEPTH, SEQ, N_HEADS, HEAD_DIM) for t in kv)
    return (xs[CTX].reshape(BATCH, SEQ, D_MODEL), xs[LAT].reshape(DEC_BATCH, DEC_SEQ, D_MODEL),
            new_k, new_v)
```
